```python
import jax, jax.numpy as jnp
from jax import lax
import numpy as np

D_MODEL = 2048
BATCH = 16
SEQ = 2048
DEPTH = 1
DEC_BATCH = 32
DEC_SEQ = 16
PAST_LEN = 2048

CHUNK = 64
Q_BLOCK = 128
HEAD_DIM = 128
SB_HEADS = 8
RET_HEADS = 8
SB_WIDTH = SB_HEADS * HEAD_DIM
RET_WIDTH = RET_HEADS * HEAD_DIM
MIX_WIDTH = SB_WIDTH + RET_WIDTH
IN_WIDTH = 4 * SB_WIDTH + 4 * RET_WIDTH
ROPE_BASE = 10000.0
EPS = 1e-6

kernel_name = "sandwich_hymba_stickbreak_retention_step"


def rms_norm(x, g):
    xf = x.astype(jnp.float32)
    y = xf * lax.rsqrt(jnp.mean(xf * xf, axis=-1, keepdims=True) + EPS)
    return (y * g.astype(jnp.float32)).astype(x.dtype)


def head_rms_norm(o, g, dtype):
    b, t, h, d = o.shape
    of = o.astype(jnp.float32)
    y = of * lax.rsqrt(jnp.mean(of * of, axis=-1, keepdims=True) + EPS)
    y = y.reshape(b, t, h * d) * g.astype(jnp.float32)
    return y.astype(dtype)


def rope(x, pos):
    half = HEAD_DIM // 2
    inv = ROPE_BASE ** (-jnp.arange(half, dtype=jnp.float32) / half)
    ang = pos.astype(jnp.float32)[:, None] * inv[None, :]
    cos = jnp.cos(ang)[None, :, None, :]
    sin = jnp.sin(ang)[None, :, None, :]
    xf = x.astype(jnp.float32)
    x1, x2 = xf[..., :half], xf[..., half:]
    out = jnp.concatenate([x1 * cos - x2 * sin, x1 * sin + x2 * cos], axis=-1)
    return out.astype(x.dtype)


def project(h, w_in, pos):
    b, t, _ = h.shape
    p = jnp.einsum('btd,de->bte', h, w_in)
    widths = [SB_WIDTH] * 4 + [RET_WIDTH] * 4
    splits = list(np.cumsum(widths)[:-1])
    q_sb, k_sb, v_sb, gate_sb, q_r, k_r, v_r, gate_r = jnp.split(p, splits, axis=-1)
    heads = lambda a, nh: a.reshape(b, t, nh, HEAD_DIM)
    q_sb, k_sb, v_sb = heads(q_sb, SB_HEADS), heads(k_sb, SB_HEADS), heads(v_sb, SB_HEADS)
    q_r = rope(heads(q_r, RET_HEADS), pos)
    k_r = rope(heads(k_r, RET_HEADS), pos) * (HEAD_DIM ** -0.5)
    v_r = heads(v_r, RET_HEADS)
    return q_sb, k_sb, v_sb, gate_sb, q_r, k_r, v_r, gate_r


def stick_breaking_block(q, k, v, q_start):
    tq, tk = q.shape[1], k.shape[1]
    z = jnp.einsum('bqhd,bkhd->bhqk', q.astype(jnp.float32), k.astype(jnp.float32)) * (HEAD_DIM ** -0.5)
    q_pos = q_start + jnp.arange(tq)
    k_pos = jnp.arange(tk)
    strict = (k_pos[None, :] < q_pos[:, None])[None, None]
    log_beta = jax.nn.log_sigmoid(z)
    log_keep = jnp.where(strict, jax.nn.log_sigmoid(-z), 0.0)
    later = lax.cumsum(log_keep, axis=3, reverse=True) - log_keep
    a = jnp.where(strict, jnp.exp(log_beta + later), 0.0)
    return jnp.einsum('bhqk,bkhd->bqhd', a.astype(v.dtype), v)


def retention_chunk(q, k, v, s, log_gamma):
    n = q.shape[1]
    idx = jnp.arange(n, dtype=jnp.float32)
    lg = log_gamma[:, None]
    decay_read = jnp.exp(lg * (idx + 1.0))
    decay_write = jnp.exp(lg * (jnp.float32(n) - 1.0 - idx))
    intra = jnp.exp(lg[:, :, None] * jnp.abs(idx[:, None] - idx[None, :]))
    qf, kf, vf = q.astype(jnp.float32), k.astype(jnp.float32), v.astype(jnp.float32)
    sf = s.astype(jnp.float32)
    scores = jnp.einsum('bihd,bjhd->bhij', qf, kf) * intra[None]
    o = jnp.einsum('bhij,bjhe->bihe', scores, vf)
    o = o + jnp.einsum('bihd,bhde->bihe', qf, sf) * decay_read.T[None, :, :, None]
    s_new = sf * jnp.exp(lg * jnp.float32(n))[None, :, :, None] + jnp.einsum('bjhd,hj,bjhe->bhde', kf, decay_write, vf)
    return o.astype(v.dtype), s_new.astype(s.dtype)


def retention_prompt(q, k, v, s0, log_gamma):
    b, t, h, d = q.shape
    nc = t // CHUNK
    to_chunks = lambda a: a.reshape(b, nc, CHUNK, h, d).transpose(1, 0, 2, 3, 4)

    def step(s, qkv):
        qc, kc, vc = qkv
        o, s = retention_chunk(qc, kc, vc, s, log_gamma)
        return s, o

    s_fin, o = lax.scan(step, s0, (to_chunks(q), to_chunks(k), to_chunks(v)))
    return o.transpose(1, 0, 2, 3, 4).reshape(b, t, h, d), s_fin


def merge(o_sb, o_r, gate_sb, gate_r, g_sb, g_r, w_out):
    y_sb = head_rms_norm(o_sb, g_sb, gate_sb.dtype) * jax.nn.silu(gate_sb)
    y_r = head_rms_norm(o_r, g_r, gate_r.dtype) * jax.nn.silu(gate_r)
    y = jnp.concatenate([y_sb, y_r], axis=-1)
    return jnp.einsum('bte,ed->btd', y, w_out)


def setup_inputs(seed: int = 0) -> dict:
    key = jax.random.key(seed)
    ks = jax.random.split(key, 11)
    f32 = jnp.float32
    nrm = jax.random.normal
    return {
        "x_prompt": nrm(ks[0], (BATCH, SEQ, D_MODEL), f32),
        "x_sample": nrm(ks[1], (DEC_BATCH, DEC_SEQ, D_MODEL), f32),
        "cache_sb_k": nrm(ks[2], (DEPTH, DEC_BATCH, PAST_LEN, SB_HEADS, HEAD_DIM), f32),
        "cache_sb_v": nrm(ks[3], (DEPTH, DEC_BATCH, PAST_LEN, SB_HEADS, HEAD_DIM), f32),
        "state_ret": nrm(ks[4], (DEPTH, DEC_BATCH, RET_HEADS, HEAD_DIM, HEAD_DIM), f32),
        "norm_pre": 1.0 + 0.05 * nrm(ks[5], (DEPTH, D_MODEL), f32),
        "w_in": nrm(ks[6], (DEPTH, D_MODEL, IN_WIDTH), f32) * (D_MODEL ** -0.5),
        "sb_head_norm": 1.0 + 0.05 * nrm(ks[7], (DEPTH, SB_WIDTH), f32),
        "ret_head_norm": 1.0 + 0.05 * nrm(ks[8], (DEPTH, RET_WIDTH), f32),
        "w_out": nrm(ks[9], (DEPTH, MIX_WIDTH, D_MODEL), f32) * (MIX_WIDTH ** -0.5),
        "norm_post": 1.0 + 0.05 * nrm(ks[10], (DEPTH, D_MODEL), f32),
    }


def reference(x_prompt, x_sample, cache_sb_k, cache_sb_v, state_ret,
              norm_pre, w_in, sb_head_norm, ret_head_norm, w_out, norm_post):
    log_gamma = jnp.log(1.0 - 2.0 ** (-5.0 - jnp.arange(RET_HEADS, dtype=jnp.float32)))
    t_p = x_prompt.shape[1]
    t_s = x_sample.shape[1]
    past = cache_sb_k.shape[2]
    pos_p = jnp.arange(t_p, dtype=jnp.int32)
    pos_s = past + jnp.arange(t_s, dtype=jnp.int32)
    xp, xs = x_prompt, x_sample
    kp_l, vp_l, sp_l, ks_l, vs_l, ss_l = [], [], [], [], [], []
    for l in range(DEPTH):
        h = rms_norm(xp, norm_pre[l])
        q_sb, k_sb, v_sb, gate_sb, q_r, k_r, v_r, gate_r = project(h, w_in[l], pos_p)
        blocks = []
        for i0 in range(0, t_p, Q_BLOCK):
            i1 = min(i0 + Q_BLOCK, t_p)
            blocks.append(stick_breaking_block(q_sb[:, i0:i1], k_sb[:, :i1], v_sb[:, :i1], i0))
        o_sb = jnp.concatenate(blocks, axis=1)
        s0 = jnp.zeros((xp.shape[0], RET_HEADS, HEAD_DIM, HEAD_DIM), state_ret.dtype)
        o_r, s_p = retention_prompt(q_r, k_r, v_r, s0, log_gamma)
        out = merge(o_sb, o_r, gate_sb, gate_r, sb_head_norm[l], ret_head_norm[l], w_out[l])
        xp = xp + rms_norm(out, norm_post[l])
        kp_l.append(k_sb)
        vp_l.append(v_sb)
        sp_l.append(s_p)
        h = rms_norm(xs, norm_pre[l])
        q_sb, k_sb, v_sb, gate_sb, q_r, k_r, v_r, gate_r = project(h, w_in[l], pos_s)
        k_all = jnp.concatenate([cache_sb_k[l].astype(k_sb.dtype), k_sb], axis=1)
        v_all = jnp.concatenate([cache_sb_v[l].astype(v_sb.dtype), v_sb], axis=1)
        o_sb = stick_breaking_block(q_sb, k_all, v_all, past)
        o_r, s_s = retention_chunk(q_r, k_r, v_r, state_ret[l], log_gamma)
        out = merge(o_sb, o_r, gate_sb, gate_r, sb_head_norm[l], ret_head_norm[l], w_out[l])
        xs = xs + rms_norm(out, norm_post[l])
        ks_l.append(k_sb)
        vs_l.append(v_sb)
        ss_l.append(s_s)
    return (xp, xs, jnp.stack(kp_l), jnp.stack(vp_l), jnp.stack(sp_l),
            jnp.stack(ks_l), jnp.stack(vs_l), jnp.stack(ss_l))
```

```python
import functools

import numpy as np
import jax
import jax.numpy as jnp
from jax import lax
from jax.experimental import pallas as pl
from jax.experimental.pallas import tpu as pltpu

HEAD_DIM = 128
N_HEADS = 8
GROUP_WIDTH = N_HEADS * HEAD_DIM
N_SEGMENTS = 8
CHUNK = 64
ROPE_BASE = 10000.0
EPS = 1e-6
Q_SCALE = HEAD_DIM ** -0.5

F32 = jnp.float32
BF16 = jnp.bfloat16

_NT = (((1,), (1,)), ((), ()))
_TN = (((0,), (0,)), ((), ()))

VMEM_LIMIT = 56 * 1024 * 1024


def _params(n_axes):
    return pltpu.CompilerParams(
        dimension_semantics=("arbitrary",) * n_axes, vmem_limit_bytes=VMEM_LIMIT)


def _log_sigmoids(z):
    l = jnp.log(1.0 + jnp.exp(-jnp.abs(z)))
    log_beta = jnp.minimum(z, 0.0) - l
    return log_beta, log_beta - z


def _head_norm_gate(o, g, gate):
    y = o * lax.rsqrt(jnp.mean(o * o, axis=-1, keepdims=True) + EPS) * g
    gf = gate.astype(F32)
    return y * (gf * jax.nn.sigmoid(gf))


IN_TN = 512
IN_COLS_PER_SEG = GROUP_WIDTH // IN_TN


def _in_proj_kernel(x_ref, g_ref, w_ref, cos_ref, sin_ref, p_ref, k_ref, v_ref, h_ref):
    j = pl.program_id(1)
    seg = j // IN_COLS_PER_SEG

    @pl.when(j == 0)
    def _():
        x = x_ref[...]
        r = lax.rsqrt(jnp.mean(x * x, axis=-1, keepdims=True) + EPS)
        h_ref[...] = (x * r * g_ref[...]).astype(BF16)

    acc = jnp.dot(h_ref[...], w_ref[...], preferred_element_type=F32)

    @pl.when(seg == 0)
    def _():
        p_ref[...] = (acc * Q_SCALE).astype(BF16)

    @pl.when(seg == 1)
    def _():
        k_ref[...] = acc
        p_ref[...] = acc.astype(BF16)

    @pl.when(seg == 2)
    def _():
        v_ref[...] = acc
        p_ref[...] = acc.astype(BF16)

    @pl.when((seg == 3) | (seg >= 6))
    def _():
        p_ref[...] = acc.astype(BF16)

    @pl.when((seg == 4) | (seg == 5))
    def _():
        scale = jnp.where(seg == 5, Q_SCALE, 1.0).astype(F32)
        cos = cos_ref[...] * scale
        sin = sin_ref[...] * scale
        for hh in range(IN_TN // HEAD_DIM):
            xs = acc[:, hh * HEAD_DIM:(hh + 1) * HEAD_DIM]
            rot = pltpu.roll(xs, HEAD_DIM // 2, 1)
            p_ref[:, hh * HEAD_DIM:(hh + 1) * HEAD_DIM] = (xs * cos + rot * sin).astype(BF16)


def _in_proj(x2d, g_pre, w_bf16, cos_t, sin_t, tm):
    m, d = x2d.shape
    n = w_bf16.shape[1]
    t_blocks = cos_t.shape[0] // tm
    n_col = n // IN_TN
    kv_col = lambda seg: (lambda i, j: (i, jnp.clip(j - seg * IN_COLS_PER_SEG, 0, IN_COLS_PER_SEG - 1)))
    return pl.pallas_call(
        _in_proj_kernel,
        grid=(m // tm, n_col),
        in_specs=[
            pl.BlockSpec((tm, d), lambda i, j: (i, 0)),
            pl.BlockSpec((1, d), lambda i, j: (0, 0)),
            pl.BlockSpec((d, IN_TN), lambda i, j: (0, j)),
            pl.BlockSpec((tm, HEAD_DIM), lambda i, j: (i % t_blocks, 0)),
            pl.BlockSpec((tm, HEAD_DIM), lambda i, j: (i % t_blocks, 0)),
        ],
        out_specs=[
            pl.BlockSpec((tm, IN_TN), lambda i, j: (i, j)),
            pl.BlockSpec((tm, IN_TN), kv_col(1)),
            pl.BlockSpec((tm, IN_TN), kv_col(2)),
        ],
        out_shape=[
            jax.ShapeDtypeStruct((m, n), BF16),
            jax.ShapeDtypeStruct((m, GROUP_WIDTH), F32),
            jax.ShapeDtypeStruct((m, GROUP_WIDTH), F32),
        ],
        scratch_shapes=[pltpu.VMEM((tm, d), BF16)],
        compiler_params=_params(2),
        name="in_proj",
    )(x2d, g_pre, w_bf16, cos_t, sin_t)


SB_T = 256


def _sb_prompt_kernel(q_ref, k_ref, v_ref, gate_ref, g_ref, o_ref):
    i = pl.program_id(2)
    q = q_ref[...]
    row = lax.broadcasted_iota(jnp.int32, (SB_T, SB_T), 0)
    col = lax.broadcasted_iota(jnp.int32, (SB_T, SB_T), 1)
    after = (row > col).astype(BF16)
    strict = col < row

    def tile(kb, carry, acc, diagonal):
        start = pl.multiple_of(kb * SB_T, SB_T)
        k = k_ref[pl.ds(start, SB_T), :]
        v = v_ref[pl.ds(start, SB_T), :]
        z = lax.dot_general(q, k, _NT, preferred_element_type=F32)
        log_beta, log_keep = _log_sigmoids(z)
        if diagonal:
            log_keep = jnp.where(strict, log_keep, 0.0)
        lk = log_keep.astype(BF16)
        later = jnp.dot(lk, after, preferred_element_type=F32)
        a = jnp.exp(log_beta + later + carry)
        if diagonal:
            a = jnp.where(strict, a, 0.0)
        acc = acc + jnp.dot(a.astype(BF16), v, preferred_element_type=F32)
        carry = carry + later[:, 0:1] + lk[:, 0:1].astype(F32)
        return carry, acc

    carry, acc = tile(i, jnp.zeros((SB_T, 1), F32), jnp.zeros((SB_T, HEAD_DIM), F32), True)

    def body(s, c):
        return tile(i - 1 - s, c[0], c[1], False)

    carry, acc = lax.fori_loop(0, i, body, (carry, acc))
    o_ref[...] = _head_norm_gate(acc, g_ref[...], gate_ref[...]).astype(BF16)


def _sb_prompt(p3, g_sb):
    b, t, _ = p3.shape
    hb = GROUP_WIDTH // HEAD_DIM
    return pl.pallas_call(
        _sb_prompt_kernel,
        grid=(b, N_HEADS, t // SB_T),
        in_specs=[
            pl.BlockSpec((None, SB_T, HEAD_DIM), lambda b_, h, i: (b_, i, h)),
            pl.BlockSpec((None, t, HEAD_DIM), lambda b_, h, i: (b_, 0, hb + h)),
            pl.BlockSpec((None, t, HEAD_DIM), lambda b_, h, i: (b_, 0, 2 * hb + h)),
            pl.BlockSpec((None, SB_T, HEAD_DIM), lambda b_, h, i: (b_, i, 3 * hb + h)),
            pl.BlockSpec((1, HEAD_DIM), lambda b_, h, i: (0, h)),
        ],
        out_specs=pl.BlockSpec((None, SB_T, HEAD_DIM), lambda b_, h, i: (b_, i, h)),
        out_shape=jax.ShapeDtypeStruct((b, t, GROUP_WIDTH), BF16),
        compiler_params=_params(3),
        name="sb_prompt",
    )(p3, p3, p3, p3, g_sb)


RET_L = 256


def _ret_prompt_kernel(lg_ref, q_ref, k_ref, v_ref, gate_ref, g_ref, o_ref, s_ref):
    h = pl.program_id(1)
    lg = lg_ref[h]
    t = q_ref.shape[0]
    ii = lax.broadcasted_iota(jnp.int32, (RET_L, RET_L), 0)
    jj = lax.broadcasted_iota(jnp.int32, (RET_L, RET_L), 1)
    dist = jnp.abs(ii - jj).astype(F32)
    w = jnp.where(jj // CHUNK <= ii // CHUNK, jnp.exp(lg * dist), 0.0)
    idx = lax.broadcasted_iota(jnp.int32, (RET_L, 1), 0).astype(F32)
    decay_read = jnp.exp(lg * (idx + 1.0))
    decay_write = jnp.exp(lg * (RET_L - 1.0 - idx))
    decay_block = jnp.exp(jnp.full((1, HEAD_DIM), lg * RET_L, F32))
    g = g_ref[...]

    def body(blk, s):
        start = pl.multiple_of(blk * RET_L, RET_L)
        qb = q_ref[pl.ds(start, RET_L), :]
        kb = k_ref[pl.ds(start, RET_L), :]
        vb = v_ref[pl.ds(start, RET_L), :]
        scores = lax.dot_general(qb, kb, _NT, preferred_element_type=F32) * w
        o = jnp.dot(scores.astype(BF16), vb, preferred_element_type=F32)
        o = o + jnp.dot(qb, s.astype(BF16), preferred_element_type=F32) * decay_read
        kw = (kb.astype(F32) * decay_write).astype(BF16)
        s = s * decay_block + lax.dot_general(kw, vb, _TN, preferred_element_type=F32)
        gate = gate_ref[pl.ds(start, RET_L), :]
        o_ref[pl.ds(start, RET_L), :] = _head_norm_gate(o, g, gate).astype(BF16)
        return s

    s_ref[...] = lax.fori_loop(0, t // RET_L, body, jnp.zeros((HEAD_DIM, HEAD_DIM), F32))


def _ret_prompt(log_gamma, p3, g_r):
    b, t, _ = p3.shape
    hb = GROUP_WIDTH // HEAD_DIM
    col = lambda seg: (lambda b_, h, lg: (b_, 0, seg * hb + h))
    return pl.pallas_call(
        _ret_prompt_kernel,
        grid_spec=pltpu.PrefetchScalarGridSpec(
            num_scalar_prefetch=1,
            grid=(b, N_HEADS),
            in_specs=[
                pl.BlockSpec((None, t, HEAD_DIM), col(4)),
                pl.BlockSpec((None, t, HEAD_DIM), col(5)),
                pl.BlockSpec((None, t, HEAD_DIM), col(6)),
                pl.BlockSpec((None, t, HEAD_DIM), col(7)),
                pl.BlockSpec((1, HEAD_DIM), lambda b_, h, lg: (0, h)),
            ],
            out_specs=[
                pl.BlockSpec((None, t, HEAD_DIM), lambda b_, h, lg: (b_, 0, h)),
                pl.BlockSpec((None, None, HEAD_DIM, HEAD_DIM), lambda b_, h, lg: (b_, h, 0, 0)),
            ],
        ),
        out_shape=[
            jax.ShapeDtypeStruct((b, t, GROUP_WIDTH), BF16),
            jax.ShapeDtypeStruct((b, N_HEADS, HEAD_DIM, HEAD_DIM), F32),
        ],
        compiler_params=_params(2),
        name="ret_prompt",
    )(log_gamma, p3, p3, p3, p3, g_r)


SBS_TK = 512


def _sb_sample_kernel(q_ref, kn_ref, vn_ref, gate_ref, g_ref, kc_ref, vc_ref, o_ref,
                      qbd_ref, acc_ref, carry_ref):
    s = pl.program_id(1)
    n = q_ref.shape[0]
    lanes = N_HEADS * n

    @pl.when(s == 0)
    def _():
        q = q_ref[...]
        qt = jnp.concatenate([q] * N_HEADS, axis=0)
        rr = lax.broadcasted_iota(jnp.int32, (lanes, GROUP_WIDTH), 0) // n
        cc = lax.broadcasted_iota(jnp.int32, (lanes, GROUP_WIDTH), 1) // HEAD_DIM
        qbd = jnp.where(rr == cc, qt, jnp.zeros_like(qt))
        qbd_ref[...] = qbd
        z = lax.dot_general(kn_ref[...], qbd, _NT, preferred_element_type=F32)
        ki = lax.broadcasted_iota(jnp.int32, (n, lanes), 0)
        qi = lax.broadcasted_iota(jnp.int32, (n, lanes), 1) % n
        strict = ki < qi
        log_beta, log_keep = _log_sigmoids(z)
        lk = jnp.where(strict, log_keep, 0.0)
        mj = lax.broadcasted_iota(jnp.int32, (n, n), 0)
        mm = lax.broadcasted_iota(jnp.int32, (n, n), 1)
        later = jnp.dot((mm > mj).astype(F32), lk, preferred_element_type=F32)
        a = jnp.where(strict, jnp.exp(log_beta + later), 0.0)
        acc_ref[...] = lax.dot_general(a.astype(BF16), vn_ref[...], _TN, preferred_element_type=F32)
        carry_ref[...] = jnp.sum(lk, axis=0, keepdims=True)

    kc = kc_ref[...].astype(BF16)
    vc = vc_ref[...].astype(BF16)
    z = lax.dot_general(kc, qbd_ref[...], _NT, preferred_element_type=F32)
    log_beta, log_keep = _log_sigmoids(z)
    lk = log_keep.astype(BF16)
    mj = lax.broadcasted_iota(jnp.int32, (SBS_TK, SBS_TK), 0)
    mm = lax.broadcasted_iota(jnp.int32, (SBS_TK, SBS_TK), 1)
    later = jnp.dot((mm > mj).astype(BF16), lk, preferred_element_type=F32)
    a = jnp.exp(log_beta + later + carry_ref[...])
    acc_ref[...] += lax.dot_general(a.astype(BF16), vc, _TN, preferred_element_type=F32)
    carry_ref[...] += later[0:1, :] + lk[0:1, :].astype(F32)

    @pl.when(s == pl.num_programs(1) - 1)
    def _():
        for hh in range(N_HEADS):
            cols = slice(hh * HEAD_DIM, (hh + 1) * HEAD_DIM)
            o = acc_ref[hh * n:(hh + 1) * n, cols]
            o_ref[:, cols] = _head_norm_gate(o, g_ref[:, cols], gate_ref[:, cols]).astype(BF16)


def _sb_sample(p3, g_sb, cache_k, cache_v):
    b, n, _ = p3.shape
    past = cache_k.shape[1]
    nkb = past // SBS_TK
    seg = lambda sidx: pl.BlockSpec((None, n, GROUP_WIDTH), lambda b_, s: (b_, 0, sidx))
    cache_spec = pl.BlockSpec((None, SBS_TK, GROUP_WIDTH), lambda b_, s: (b_, nkb - 1 - s, 0))
    return pl.pallas_call(
        _sb_sample_kernel,
        grid=(b, nkb),
        in_specs=[seg(0), seg(1), seg(2), seg(3),
                  pl.BlockSpec((1, GROUP_WIDTH), lambda b_, s: (0, 0)),
                  cache_spec, cache_spec],
        out_specs=pl.BlockSpec((None, n, GROUP_WIDTH), lambda b_, s: (b_, 0, 0)),
        out_shape=jax.ShapeDtypeStruct((b, n, GROUP_WIDTH), BF16),
        scratch_shapes=[
            pltpu.VMEM((N_HEADS * n, GROUP_WIDTH), BF16),
            pltpu.VMEM((N_HEADS * n, GROUP_WIDTH), F32),
            pltpu.VMEM((1, N_HEADS * n), F32),
        ],
        compiler_params=_params(2),
        name="sb_sample",
    )(p3, p3, p3, p3, g_sb, cache_k, cache_v)


def _ret_sample_kernel(lg_ref, q_ref, k_ref, v_ref, gate_ref, g_ref, s_ref, o_ref, so_ref):
    n = q_ref.shape[0]
    ii = lax.broadcasted_iota(jnp.int32, (n, n), 0)
    jj = lax.broadcasted_iota(jnp.int32, (n, n), 1)
    dist = jnp.abs(ii - jj).astype(F32)
    idx = lax.broadcasted_iota(jnp.int32, (n, 1), 0).astype(F32)
    for hh in range(N_HEADS):
        lg = lg_ref[hh]
        cols = slice(hh * HEAD_DIM, (hh + 1) * HEAD_DIM)
        qh, kh, vh = q_ref[:, cols], k_ref[:, cols], v_ref[:, cols]
        s = s_ref[hh]
        scores = lax.dot_general(qh, kh, _NT, preferred_element_type=F32) * jnp.exp(lg * dist)
        o = jnp.dot(scores.astype(BF16), vh, preferred_element_type=F32)
        o = o + jnp.dot(qh, s.astype(BF16), preferred_element_type=F32) * jnp.exp(lg * (idx + 1.0))
        kw = (kh.astype(F32) * jnp.exp(lg * (n - 1.0 - idx))).astype(BF16)
        decay_all = jnp.exp(jnp.full((1, HEAD_DIM), lg * n, F32))
        so_ref[hh] = s * decay_all + lax.dot_general(kw, vh, _TN, preferred_element_type=F32)
        o_ref[:, cols] = _head_norm_gate(o, g_ref[:, cols], gate_ref[:, cols]).astype(BF16)


def _ret_sample(log_gamma, p3, g_r, state):
    b, n, _ = p3.shape
    seg = lambda sidx: pl.BlockSpec((None, n, GROUP_WIDTH), lambda b_, lg: (b_, 0, sidx))
    state_spec = pl.BlockSpec((None, N_HEADS, HEAD_DIM, HEAD_DIM), lambda b_, lg: (b_, 0, 0, 0))
    return pl.pallas_call(
        _ret_sample_kernel,
        grid_spec=pltpu.PrefetchScalarGridSpec(
            num_scalar_prefetch=1,
            grid=(b,),
            in_specs=[seg(4), seg(5), seg(6), seg(7),
                      pl.BlockSpec((1, GROUP_WIDTH), lambda b_, lg: (0, 0)),
                      state_spec],
            out_specs=[pl.BlockSpec((None, n, GROUP_WIDTH), lambda b_, lg: (b_, 0, 0)), state_spec],
        ),
        out_shape=[
            jax.ShapeDtypeStruct((b, n, GROUP_WIDTH), BF16),
            jax.ShapeDtypeStruct(state.shape, F32),
        ],
        compiler_params=_params(1),
        name="ret_sample",
    )(log_gamma, p3, p3, p3, p3, g_r, state)


OUT_TM = 512


def _out_proj_kernel(ysb_ref, yr_ref, w_ref, g_ref, x_ref, o_ref):
    out = jnp.dot(ysb_ref[...], w_ref[0:GROUP_WIDTH, :], preferred_element_type=F32)
    out = out + jnp.dot(yr_ref[...], w_ref[GROUP_WIDTH:2 * GROUP_WIDTH, :], preferred_element_type=F32)
    r = lax.rsqrt(jnp.mean(out * out, axis=-1, keepdims=True) + EPS)
    o_ref[...] = x_ref[...] + out * r * g_ref[...]


def _out_proj(y_sb, y_r, w_bf16, g_post, x2d):
    m, d = x2d.shape
    tm = min(OUT_TM, m)
    row = lambda width: pl.BlockSpec((tm, width), lambda i: (i, 0))
    return pl.pallas_call(
        _out_proj_kernel,
        grid=(m // tm,),
        in_specs=[row(GROUP_WIDTH), row(GROUP_WIDTH),
                  pl.BlockSpec(w_bf16.shape, lambda i: (0, 0)),
                  pl.BlockSpec((1, d), lambda i: (0, 0)),
                  row(d)],
        out_specs=row(d),
        out_shape=jax.ShapeDtypeStruct((m, d), F32),
        compiler_params=_params(1),
        name="out_proj",
    )(y_sb, y_r, w_bf16, g_post, x2d)


def _rope_tables(positions):
    half = HEAD_DIM // 2
    inv = ROPE_BASE ** (-np.arange(half, dtype=np.float64) / half)
    ang = positions.astype(np.float64)[:, None] * inv[None, :]
    cos = np.concatenate([np.cos(ang), np.cos(ang)], axis=-1)
    sin = np.concatenate([-np.sin(ang), np.sin(ang)], axis=-1)
    return jnp.asarray(cos, F32), jnp.asarray(sin, F32)


IN_TM_PROMPT = 1024


def kernel(x_prompt, x_sample, cache_sb_k, cache_sb_v, state_ret, norm_pre, w_in, sb_head_norm,
           ret_head_norm, w_out, norm_post):
    depth = norm_pre.shape[0]
    assert depth == 1
    b_p, t_p, d = x_prompt.shape
    b_s, t_s, _ = x_sample.shape
    past = cache_sb_k.shape[2]
    log_gamma = jnp.asarray(np.log(1.0 - 2.0 ** (-5.0 - np.arange(N_HEADS))), F32)

    l = 0
    w_in_b = w_in[l].astype(BF16)
    w_out_b = w_out[l].astype(BF16)
    g_pre = norm_pre[l][None, :]
    g_post = norm_post[l][None, :]
    g_sb = sb_head_norm[l][None, :]
    g_r = ret_head_norm[l][None, :]

    cos_p, sin_p = _rope_tables(np.arange(t_p))
    xp2 = x_prompt.reshape(b_p * t_p, d)
    p_p, k_p, v_p = _in_proj(xp2, g_pre, w_in_b, cos_p, sin_p, IN_TM_PROMPT)
    p3 = p_p.reshape(b_p, t_p, N_SEGMENTS * GROUP_WIDTH)
    y_sb = _sb_prompt(p3, g_sb)
    y_r, s_p = _ret_prompt(log_gamma, p3, g_r)
    y_p = _out_proj(y_sb.reshape(b_p * t_p, GROUP_WIDTH), y_r.reshape(b_p * t_p, GROUP_WIDTH),
                    w_out_b, g_post, xp2).reshape(b_p, t_p, d)

    cos_s, sin_s = _rope_tables(np.tile(past + np.arange(t_s), b_s))
    xs2 = x_sample.reshape(b_s * t_s, d)
    p_s, k_s, v_s = _in_proj(xs2, g_pre, w_in_b, cos_s, sin_s, b_s * t_s)
    ps3 = p_s.reshape(b_s, t_s, N_SEGMENTS * GROUP_WIDTH)
    ys_sb = _sb_sample(ps3, g_sb, cache_sb_k[l].reshape(b_s, past, GROUP_WIDTH),
                       cache_sb_v[l].reshape(b_s, past, GROUP_WIDTH))
    ys_r, s_s = _ret_sample(log_gamma, ps3, g_r, state_ret[l])
    y_s = _out_proj(ys_sb.reshape(b_s * t_s, GROUP_WIDTH), ys_r.reshape(b_s * t_s, GROUP_WIDTH),
                    w_out_b, g_post, xs2).reshape(b_s, t_s, d)

    heads = lambda a, bb, tt: a.reshape(1, bb, tt, N_HEADS, HEAD_DIM)
    return (y_p, y_s, heads(k_p, b_p, t_p), heads(v_p, b_p, t_p), s_p[None],
            heads(k_s, b_s, t_s), heads(v_s, b_s, t_s), s_s[None])
```

```python
import functools

import numpy as np
import jax
import jax.numpy as jnp
from jax import lax
from jax.experimental import pallas as pl
from jax.experimental.pallas import tpu as pltpu

HEAD_DIM = 128
N_HEADS = 8
GROUP_WIDTH = N_HEADS * HEAD_DIM
N_SEGMENTS = 8
CHUNK = 64
ROPE_BASE = 10000.0
EPS = 1e-6
Q_SCALE = HEAD_DIM ** -0.5

F32 = jnp.float32
BF16 = jnp.bfloat16

_NT = (((1,), (1,)), ((), ()))
_TN = (((0,), (0,)), ((), ()))

VMEM_LIMIT = 56 * 1024 * 1024


def _params(n_axes):
    return pltpu.CompilerParams(
        dimension_semantics=("arbitrary",) * n_axes, vmem_limit_bytes=VMEM_LIMIT)


def _log_sigmoids(z):
    l = jnp.log(1.0 + jnp.exp(-jnp.abs(z)))
    log_beta = jnp.minimum(z, 0.0) - l
    return log_beta, log_beta - z


def _head_norm_gate(o, g, gate):
    y = o * lax.rsqrt(jnp.mean(o * o, axis=-1, keepdims=True) + EPS) * g
    gf = gate.astype(F32)
    return y * (gf * jax.nn.sigmoid(gf))


IN_TN = 512
IN_COLS_PER_SEG = GROUP_WIDTH // IN_TN


def _in_proj_kernel(x_ref, g_ref, w_ref, cos_ref, sin_ref, p_ref, k_ref, v_ref, h_ref):
    j = pl.program_id(1)
    seg = j // IN_COLS_PER_SEG

    @pl.when(j == 0)
    def _():
        x = x_ref[...]
        r = lax.rsqrt(jnp.mean(x * x, axis=-1, keepdims=True) + EPS)
        h_ref[...] = (x * r * g_ref[...]).astype(BF16)

    acc = jnp.dot(h_ref[...], w_ref[...], preferred_element_type=F32)

    rotary = (seg == 4) | (seg == 5)
    head_cols = [slice(hh * HEAD_DIM, (hh + 1) * HEAD_DIM) for hh in range(IN_TN // HEAD_DIM)]

    @pl.when(rotary)
    def _():
        for cols in head_cols:
            xs = acc[:, cols]
            rot = pltpu.roll(xs, HEAD_DIM // 2, 1)
            p_ref[:, cols] = (xs * cos_ref[...] + rot * sin_ref[...]).astype(BF16)

    @pl.when(jnp.logical_not(rotary))
    def _():
        for cols in head_cols:
            p_ref[:, cols] = (acc[:, cols] * cos_ref[...]).astype(BF16)

    @pl.when(seg == 1)
    def _():
        k_ref[...] = acc

    @pl.when(seg == 2)
    def _():
        v_ref[...] = acc


def _table_index(j):
    seg = j // IN_COLS_PER_SEG
    return jnp.where(seg == 0, 0, jnp.where(seg == 4, 2, jnp.where(seg == 5, 3, 1)))


def _in_proj(x2d, g_pre, w_bf16, cos_t, sin_t, tm):
    m, d = x2d.shape
    n = w_bf16.shape[1]
    t_blocks = cos_t.shape[1] // tm
    n_col = n // IN_TN
    kv_col = lambda seg: (lambda i, j: (i, jnp.clip(j - seg * IN_COLS_PER_SEG, 0, IN_COLS_PER_SEG - 1)))
    table_spec = pl.BlockSpec((None, tm, HEAD_DIM), lambda i, j: (_table_index(j), i % t_blocks, 0))
    return pl.pallas_call(
        _in_proj_kernel,
        grid=(m // tm, n_col),
        in_specs=[
            pl.BlockSpec((tm, d), lambda i, j: (i, 0)),
            pl.BlockSpec((1, d), lambda i, j: (0, 0)),
            pl.BlockSpec((d, IN_TN), lambda i, j: (0, j)),
            table_spec,
            table_spec,
        ],
        out_specs=[
            pl.BlockSpec((tm, IN_TN), lambda i, j: (i, j)),
            pl.BlockSpec((tm, IN_TN), kv_col(1)),
            pl.BlockSpec((tm, IN_TN), kv_col(2)),
        ],
        out_shape=[
            jax.ShapeDtypeStruct((m, n), BF16),
            jax.ShapeDtypeStruct((m, GROUP_WIDTH), F32),
            jax.ShapeDtypeStruct((m, GROUP_WIDTH), F32),
        ],
        scratch_shapes=[pltpu.VMEM((tm, d), BF16)],
        compiler_params=_params(2),
        name="in_proj",
    )(x2d, g_pre, w_bf16, cos_t, sin_t)


SB_T = 256
SB_HEADS_PER_STEP = 8
EXP_UNDERFLOW = -104.0


def _sb_prompt_kernel(q_ref, k_ref, v_ref, gate_ref, g_ref, o_ref, vt_ref):
    i = pl.program_id(2)
    n_kb = vt_ref.shape[0]
    heads = range(SB_HEADS_PER_STEP)
    lanes = lambda hh: slice(hh * HEAD_DIM, (hh + 1) * HEAD_DIM)

    @pl.when(i == 0)
    def _():
        for kb in range(n_kb):
            vt_ref[kb] = v_ref[kb * SB_T:(kb + 1) * SB_T, :].astype(F32).T.astype(BF16)

    row = lax.broadcasted_iota(jnp.int32, (SB_T, SB_T), 0)
    col = lax.broadcasted_iota(jnp.int32, (SB_T, SB_T), 1)
    later_keys = (col > row).astype(BF16)
    strict = row < col

    def tiles(kb, carries, accs, diagonal):
        start = pl.multiple_of(kb * SB_T, SB_T)
        zs = [lax.dot_general(k_ref[pl.ds(start, SB_T), lanes(hh)], q_ref[:, lanes(hh)], _NT,
                              preferred_element_type=F32) for hh in heads]
        log_betas, lks = [], []
        for z in zs:
            log_beta, log_keep = _log_sigmoids(z.astype(BF16))
            if diagonal:
                log_keep = jnp.where(strict, log_keep, jnp.zeros_like(log_keep))
            log_betas.append(log_beta)
            lks.append(log_keep)
        laters = [jnp.dot(later_keys, lk, preferred_element_type=F32) for lk in lks]
        probs = []
        for hh in heads:
            a = jnp.exp(log_betas[hh] + laters[hh] + carries[hh])
            if diagonal:
                a = jnp.where(strict, a, 0.0)
            probs.append(a.astype(BF16))
        accs = [accs[hh] + jnp.dot(vt_ref[kb, lanes(hh), :], probs[hh], preferred_element_type=F32)
                for hh in heads]
        carries = [carries[hh] + laters[hh][0:1, :] + lks[hh][0:1, :].astype(F32) for hh in heads]
        return carries, accs

    def alive(carries):
        return functools.reduce(jnp.maximum, [jnp.max(c) for c in carries]) > EXP_UNDERFLOW

    carries, accs = tiles(i, [jnp.zeros((1, SB_T), F32) for _ in heads],
                          [jnp.zeros((HEAD_DIM, SB_T), F32) for _ in heads], True)

    def cond(c):
        return (c[0] < i) & c[1]

    def body(c):
        s, _, carries, accs = c
        carries, accs = tiles(i - 1 - s, carries, accs, False)
        return s + 1, alive(carries), carries, accs

    _, _, _, accs = lax.while_loop(cond, body, (jnp.int32(0), alive(carries), carries, accs))
    for hh in heads:
        o = accs[hh].T
        o_ref[:, lanes(hh)] = _head_norm_gate(o, g_ref[:, lanes(hh)], gate_ref[:, lanes(hh)]).astype(BF16)


def _sb_prompt(p3, g_sb):
    b, t, _ = p3.shape
    w = SB_HEADS_PER_STEP * HEAD_DIM
    sb = GROUP_WIDTH // w
    return pl.pallas_call(
        _sb_prompt_kernel,
        grid=(b, sb, t // SB_T),
        in_specs=[
            pl.BlockSpec((None, SB_T, w), lambda b_, h, i: (b_, i, h)),
            pl.BlockSpec((None, t, w), lambda b_, h, i: (b_, 0, sb + h)),
            pl.BlockSpec((None, t, w), lambda b_, h, i: (b_, 0, 2 * sb + h)),
            pl.BlockSpec((None, SB_T, w), lambda b_, h, i: (b_, i, 3 * sb + h)),
            pl.BlockSpec((1, w), lambda b_, h, i: (0, h)),
        ],
        out_specs=pl.BlockSpec((None, SB_T, w), lambda b_, h, i: (b_, i, h)),
        out_shape=jax.ShapeDtypeStruct((b, t, GROUP_WIDTH), BF16),
        scratch_shapes=[pltpu.VMEM((t // SB_T, w, SB_T), BF16)],
        compiler_params=_params(3),
        name="sb_prompt",
    )(p3, p3, p3, p3, g_sb)


RET_L = 256


def _ret_prompt_kernel(lg_ref, q_ref, k_ref, v_ref, gate_ref, g_ref, o_ref, s_ref):
    h = pl.program_id(1)
    lg = lg_ref[h]
    t = q_ref.shape[0]
    ii = lax.broadcasted_iota(jnp.int32, (RET_L, RET_L), 0)
    jj = lax.broadcasted_iota(jnp.int32, (RET_L, RET_L), 1)
    dist = jnp.abs(ii - jj).astype(F32)
    w = jnp.where(jj // CHUNK <= ii // CHUNK, jnp.exp(lg * dist), 0.0)
    idx = lax.broadcasted_iota(jnp.int32, (RET_L, 1), 0).astype(F32)
    decay_read = jnp.exp(lg * (idx + 1.0))
    decay_write = jnp.exp(lg * (RET_L - 1.0 - idx))
    decay_block = jnp.exp(jnp.full((1, HEAD_DIM), lg * RET_L, F32))
    g = g_ref[...]

    def body(blk, s):
        start = pl.multiple_of(blk * RET_L, RET_L)
        qb = q_ref[pl.ds(start, RET_L), :]
        kb = k_ref[pl.ds(start, RET_L), :]
        vb = v_ref[pl.ds(start, RET_L), :]
        scores = lax.dot_general(qb, kb, _NT, preferred_element_type=F32) * w
        o = jnp.dot(scores.astype(BF16), vb, preferred_element_type=F32)
        o = o + jnp.dot(qb, s.astype(BF16), preferred_element_type=F32) * decay_read
        kw = (kb.astype(F32) * decay_write).astype(BF16)
        s = s * decay_block + lax.dot_general(kw, vb, _TN, preferred_element_type=F32)
        gate = gate_ref[pl.ds(start, RET_L), :]
        o_ref[pl.ds(start, RET_L), :] = _head_norm_gate(o, g, gate).astype(BF16)
        return s

    s_ref[...] = lax.fori_loop(0, t // RET_L, body, jnp.zeros((HEAD_DIM, HEAD_DIM), F32))


def _ret_prompt(log_gamma, p3, g_r):
    b, t, _ = p3.shape
    hb = GROUP_WIDTH // HEAD_DIM
    col = lambda seg: (lambda b_, h, lg: (b_, 0, seg * hb + h))
    return pl.pallas_call(
        _ret_prompt_kernel,
        grid_spec=pltpu.PrefetchScalarGridSpec(
            num_scalar_prefetch=1,
            grid=(b, N_HEADS),
            in_specs=[
                pl.BlockSpec((None, t, HEAD_DIM), col(4)),
                pl.BlockSpec((None, t, HEAD_DIM), col(5)),
                pl.BlockSpec((None, t, HEAD_DIM), col(6)),
                pl.BlockSpec((None, t, HEAD_DIM), col(7)),
                pl.BlockSpec((1, HEAD_DIM), lambda b_, h, lg: (0, h)),
            ],
            out_specs=[
                pl.BlockSpec((None, t, HEAD_DIM), lambda b_, h, lg: (b_, 0, h)),
                pl.BlockSpec((None, None, HEAD_DIM, HEAD_DIM), lambda b_, h, lg: (b_, h, 0, 0)),
            ],
        ),
        out_shape=[
            jax.ShapeDtypeStruct((b, t, GROUP_WIDTH), BF16),
            jax.ShapeDtypeStruct((b, N_HEADS, HEAD_DIM, HEAD_DIM), F32),
        ],
        compiler_params=_params(2),
        name="ret_prompt",
    )(log_gamma, p3, p3, p3, p3, g_r)


SBS_TK = 512


def _sb_sample_kernel(q_ref, kn_ref, vn_ref, gate_ref, g_ref, kc_ref, vc_ref, o_ref,
                      qbd_ref, acc_ref, carry_ref):
    s = pl.program_id(1)
    n = q_ref.shape[0]
    lanes = N_HEADS * n

    @pl.when(s == 0)
    def _():
        q = q_ref[...]
        qt = jnp.concatenate([q] * N_HEADS, axis=0)
        rr = lax.broadcasted_iota(jnp.int32, (lanes, GROUP_WIDTH), 0) // n
        cc = lax.broadcasted_iota(jnp.int32, (lanes, GROUP_WIDTH), 1) // HEAD_DIM
        qbd = jnp.where(rr == cc, qt, jnp.zeros_like(qt))
        qbd_ref[...] = qbd
        z = lax.dot_general(kn_ref[...], qbd, _NT, preferred_element_type=F32)
        ki = lax.broadcasted_iota(jnp.int32, (n, lanes), 0)
        qi = lax.broadcasted_iota(jnp.int32, (n, lanes), 1) % n
        strict = ki < qi
        log_beta, log_keep = _log_sigmoids(z)
        lk = jnp.where(strict, log_keep, 0.0)
        mj = lax.broadcasted_iota(jnp.int32, (n, n), 0)
        mm = lax.broadcasted_iota(jnp.int32, (n, n), 1)
        later = jnp.dot((mm > mj).astype(F32), lk, preferred_element_type=F32)
        a = jnp.where(strict, jnp.exp(log_beta + later), 0.0)
        acc_ref[...] = lax.dot_general(a.astype(BF16), vn_ref[...], _TN, preferred_element_type=F32)
        carry_ref[...] = jnp.sum(lk, axis=0, keepdims=True)

    def all_heads(ref):
        return jnp.concatenate(
            [ref[pl.ds(hh, SBS_TK, stride=N_HEADS), :].astype(BF16) for hh in range(N_HEADS)], axis=1)

    kc = all_heads(kc_ref)
    vc = all_heads(vc_ref)
    z = lax.dot_general(kc, qbd_ref[...], _NT, preferred_element_type=F32)
    log_beta, log_keep = _log_sigmoids(z)
    lk = log_keep.astype(BF16)
    mj = lax.broadcasted_iota(jnp.int32, (SBS_TK, SBS_TK), 0)
    mm = lax.broadcasted_iota(jnp.int32, (SBS_TK, SBS_TK), 1)
    later = jnp.dot((mm > mj).astype(BF16), lk, preferred_element_type=F32)
    a = jnp.exp(log_beta + later + carry_ref[...])
    acc_ref[...] += lax.dot_general(a.astype(BF16), vc, _TN, preferred_element_type=F32)
    carry_ref[...] += later[0:1, :] + lk[0:1, :].astype(F32)

    @pl.when(s == pl.num_programs(1) - 1)
    def _():
        for hh in range(N_HEADS):
            cols = slice(hh * HEAD_DIM, (hh + 1) * HEAD_DIM)
            o = acc_ref[hh * n:(hh + 1) * n, cols]
            o_ref[:, cols] = _head_norm_gate(o, g_ref[:, cols], gate_ref[:, cols]).astype(BF16)


def _sb_sample(p3, g_sb, cache_k, cache_v):
    b, n, _ = p3.shape
    nkb = cache_k.shape[1] // (SBS_TK * N_HEADS)
    seg = lambda sidx: pl.BlockSpec((None, n, GROUP_WIDTH), lambda b_, s: (b_, 0, sidx))
    cache_spec = pl.BlockSpec((None, SBS_TK * N_HEADS, HEAD_DIM), lambda b_, s: (b_, nkb - 1 - s, 0))
    return pl.pallas_call(
        _sb_sample_kernel,
        grid=(b, nkb),
        in_specs=[seg(0), seg(1), seg(2), seg(3),
                  pl.BlockSpec((1, GROUP_WIDTH), lambda b_, s: (0, 0)),
                  cache_spec, cache_spec],
        out_specs=pl.BlockSpec((None, n, GROUP_WIDTH), lambda b_, s: (b_, 0, 0)),
        out_shape=jax.ShapeDtypeStruct((b, n, GROUP_WIDTH), BF16),
        scratch_shapes=[
            pltpu.VMEM((N_HEADS * n, GROUP_WIDTH), BF16),
            pltpu.VMEM((N_HEADS * n, GROUP_WIDTH), F32),
            pltpu.VMEM((1, N_HEADS * n), F32),
        ],
        compiler_params=_params(2),
        name="sb_sample",
    )(p3, p3, p3, p3, g_sb, cache_k, cache_v)


def _ret_sample_kernel(lg_ref, q_ref, k_ref, v_ref, gate_ref, g_ref, s_ref, o_ref, so_ref):
    n = q_ref.shape[0]
    ii = lax.broadcasted_iota(jnp.int32, (n, n), 0)
    jj = lax.broadcasted_iota(jnp.int32, (n, n), 1)
    dist = jnp.abs(ii - jj).astype(F32)
    idx = lax.broadcasted_iota(jnp.int32, (n, 1), 0).astype(F32)
    for hh in range(N_HEADS):
        lg = lg_ref[hh]
        cols = slice(hh * HEAD_DIM, (hh + 1) * HEAD_DIM)
        qh, kh, vh = q_ref[:, cols], k_ref[:, cols], v_ref[:, cols]
        s = s_ref[hh]
        scores = lax.dot_general(qh, kh, _NT, preferred_element_type=F32) * jnp.exp(lg * dist)
        o = jnp.dot(scores.astype(BF16), vh, preferred_element_type=F32)
        o = o + jnp.dot(qh, s.astype(BF16), preferred_element_type=F32) * jnp.exp(lg * (idx + 1.0))
        kw = (kh.astype(F32) * jnp.exp(lg * (n - 1.0 - idx))).astype(BF16)
        decay_all = jnp.exp(jnp.full((1, HEAD_DIM), lg * n, F32))
        so_ref[hh] = s * decay_all + lax.dot_general(kw, vh, _TN, preferred_element_type=F32)
        o_ref[:, cols] = _head_norm_gate(o, g_ref[:, cols], gate_ref[:, cols]).astype(BF16)


def _ret_sample(log_gamma, p3, g_r, state):
    b, n, _ = p3.shape
    seg = lambda sidx: pl.BlockSpec((None, n, GROUP_WIDTH), lambda b_, lg: (b_, 0, sidx))
    state_spec = pl.BlockSpec((None, N_HEADS, HEAD_DIM, HEAD_DIM), lambda b_, lg: (b_, 0, 0, 0))
    return pl.pallas_call(
        _ret_sample_kernel,
        grid_spec=pltpu.PrefetchScalarGridSpec(
            num_scalar_prefetch=1,
            grid=(b,),
            in_specs=[seg(4), seg(5), seg(6), seg(7),
                      pl.BlockSpec((1, GROUP_WIDTH), lambda b_, lg: (0, 0)),
                      state_spec],
            out_specs=[pl.BlockSpec((None, n, GROUP_WIDTH), lambda b_, lg: (b_, 0, 0)), state_spec],
        ),
        out_shape=[
            jax.ShapeDtypeStruct((b, n, GROUP_WIDTH), BF16),
            jax.ShapeDtypeStruct(state.shape, F32),
        ],
        compiler_params=_params(1),
        name="ret_sample",
    )(log_gamma, p3, p3, p3, p3, g_r, state)


OUT_TM = 512


def _out_proj_kernel(ysb_ref, yr_ref, w_ref, g_ref, x_ref, o_ref):
    out = jnp.dot(ysb_ref[...], w_ref[0:GROUP_WIDTH, :], preferred_element_type=F32)
    out = out + jnp.dot(yr_ref[...], w_ref[GROUP_WIDTH:2 * GROUP_WIDTH, :], preferred_element_type=F32)
    r = lax.rsqrt(jnp.mean(out * out, axis=-1, keepdims=True) + EPS)
    o_ref[...] = x_ref[...] + out * r * g_ref[...]


def _out_proj(y_sb, y_r, w_bf16, g_post, x2d):
    m, d = x2d.shape
    tm = min(OUT_TM, m)
    row = lambda width: pl.BlockSpec((tm, width), lambda i: (i, 0))
    return pl.pallas_call(
        _out_proj_kernel,
        grid=(m // tm,),
        in_specs=[row(GROUP_WIDTH), row(GROUP_WIDTH),
                  pl.BlockSpec(w_bf16.shape, lambda i: (0, 0)),
                  pl.BlockSpec((1, d), lambda i: (0, 0)),
                  row(d)],
        out_specs=row(d),
        out_shape=jax.ShapeDtypeStruct((m, d), F32),
        compiler_params=_params(1),
        name="out_proj",
    )(y_sb, y_r, w_bf16, g_post, x2d)


def _epilogue_tables(positions):
    half = HEAD_DIM // 2
    inv = ROPE_BASE ** (-np.arange(half, dtype=np.float64) / half)
    ang = positions.astype(np.float64)[:, None] * inv[None, :]
    cos = np.concatenate([np.cos(ang), np.cos(ang)], axis=-1)
    sin = np.concatenate([-np.sin(ang), np.sin(ang)], axis=-1)
    one, zero = np.ones_like(cos), np.zeros_like(sin)
    cos4 = np.stack([one * Q_SCALE, one, cos, cos * Q_SCALE])
    sin4 = np.stack([zero, zero, sin, sin * Q_SCALE])
    return jnp.asarray(cos4, F32), jnp.asarray(sin4, F32)


IN_TM_PROMPT = 1024


def kernel(x_prompt, x_sample, cache_sb_k, cache_sb_v, state_ret, norm_pre, w_in, sb_head_norm,
           ret_head_norm, w_out, norm_post):
    depth = norm_pre.shape[0]
    assert depth == 1
    b_p, t_p, d = x_prompt.shape
    b_s, t_s, _ = x_sample.shape
    past = cache_sb_k.shape[2]
    log_gamma = jnp.asarray(np.log(1.0 - 2.0 ** (-5.0 - np.arange(N_HEADS))), F32)

    l = 0
    w_in_b = w_in[l].astype(BF16)
    w_out_b = w_out[l].astype(BF16)
    g_pre = norm_pre[l][None, :]
    g_post = norm_post[l][None, :]
    g_sb = sb_head_norm[l][None, :]
    g_r = ret_head_norm[l][None, :]

    cos_p, sin_p = _epilogue_tables(np.arange(t_p))
    xp2 = x_prompt.reshape(b_p * t_p, d)
    p_p, k_p, v_p = _in_proj(xp2, g_pre, w_in_b, cos_p, sin_p, IN_TM_PROMPT)
    p3 = p_p.reshape(b_p, t_p, N_SEGMENTS * GROUP_WIDTH)
    y_sb = _sb_prompt(p3, g_sb)
    y_r, s_p = _ret_prompt(log_gamma, p3, g_r)
    y_p = _out_proj(y_sb.reshape(b_p * t_p, GROUP_WIDTH), y_r.reshape(b_p * t_p, GROUP_WIDTH),
                    w_out_b, g_post, xp2).reshape(b_p, t_p, d)

    cos_s, sin_s = _epilogue_tables(np.tile(past + np.arange(t_s), b_s))
    xs2 = x_sample.reshape(b_s * t_s, d)
    p_s, k_s, v_s = _in_proj(xs2, g_pre, w_in_b, cos_s, sin_s, b_s * t_s)
    ps3 = p_s.reshape(b_s, t_s, N_SEGMENTS * GROUP_WIDTH)
    ys_sb = _sb_sample(ps3, g_sb, cache_sb_k[l].reshape(b_s, past * N_HEADS, HEAD_DIM),
                       cache_sb_v[l].reshape(b_s, past * N_HEADS, HEAD_DIM))
    ys_r, s_s = _ret_sample(log_gamma, ps3, g_r, state_ret[l])
    y_s = _out_proj(ys_sb.reshape(b_s * t_s, GROUP_WIDTH), ys_r.reshape(b_s * t_s, GROUP_WIDTH),
                    w_out_b, g_post, xs2).reshape(b_s, t_s, d)

    heads = lambda a, bb, tt: a.reshape(1, bb, tt, N_HEADS, HEAD_DIM)
    return (y_p, y_s, heads(k_p, b_p, t_p), heads(v_p, b_p, t_p), s_p[None],
            heads(k_s, b_s, t_s), heads(v_s, b_s, t_s), s_s[None])
```

```python
import functools

import numpy as np
import jax
import jax.numpy as jnp
from jax import lax
from jax.experimental import pallas as pl
from jax.experimental.pallas import tpu as pltpu

HEAD_DIM = 128
N_HEADS = 8
GROUP_WIDTH = N_HEADS * HEAD_DIM
N_SEGMENTS = 8
CHUNK = 64
ROPE_BASE = 10000.0
EPS = 1e-6
Q_SCALE = HEAD_DIM ** -0.5

F32 = jnp.float32
BF16 = jnp.bfloat16

_NT = (((1,), (1,)), ((), ()))
_TN = (((0,), (0,)), ((), ()))

VMEM_LIMIT = 56 * 1024 * 1024


def _params(n_axes):
    return pltpu.CompilerParams(
        dimension_semantics=("arbitrary",) * n_axes, vmem_limit_bytes=VMEM_LIMIT)


def _log_sigmoids(z):
    l = jnp.log(1.0 + jnp.exp(-jnp.abs(z)))
    log_beta = jnp.minimum(z, 0.0) - l
    return log_beta, log_beta - z


def _head_norm_gate(o, g, gate):
    y = o * lax.rsqrt(jnp.mean(o * o, axis=-1, keepdims=True) + EPS) * g
    gf = gate.astype(F32)
    return y * (gf * jax.nn.sigmoid(gf))


IN_TN = 512
IN_COLS_PER_SEG = GROUP_WIDTH // IN_TN
IN_ROW_SLICE = 256


def _in_proj_kernel(x_ref, g_ref, w_ref, cos_ref, sin_ref, p_ref, k_ref, v_ref, h_ref):
    j = pl.program_id(1)
    seg = j // IN_COLS_PER_SEG

    @pl.when(j == 0)
    def _():
        x = x_ref[...]
        r = lax.rsqrt(jnp.mean(x * x, axis=-1, keepdims=True) + EPS)
        h_ref[...] = (x * r * g_ref[...]).astype(BF16)

    head_cols = [slice(hh * HEAD_DIM, (hh + 1) * HEAD_DIM) for hh in range(IN_TN // HEAD_DIM)]
    tm = h_ref.shape[0]
    row_slices = [slice(r, r + IN_ROW_SLICE) for r in range(0, tm, IN_ROW_SLICE)]

    def project(rotary, f32_out):
        for rows in row_slices:
            acc = jnp.dot(h_ref[rows, :], w_ref[...], preferred_element_type=F32)
            cos = cos_ref[rows, :]
            for cols in head_cols:
                xs = acc[:, cols]
                y = xs * cos
                if rotary:
                    y = y + pltpu.roll(xs, HEAD_DIM // 2, 1) * sin_ref[rows, :]
                p_ref[rows, cols] = y.astype(BF16)
            if f32_out is not None:
                f32_out[rows, :] = acc

    pl.when((seg == 0) | (seg == 3) | (seg >= 6))(lambda: project(False, None))
    pl.when(seg == 1)(lambda: project(False, k_ref))
    pl.when(seg == 2)(lambda: project(False, v_ref))
    pl.when((seg == 4) | (seg == 5))(lambda: project(True, None))


def _table_index(j):
    seg = j // IN_COLS_PER_SEG
    return jnp.where(seg == 0, 0, jnp.where(seg == 4, 2, jnp.where(seg == 5, 3, 1)))


def _in_proj(x2d, g_pre, w_bf16, cos_t, sin_t, tm):
    m, d = x2d.shape
    n = w_bf16.shape[1]
    t_blocks = cos_t.shape[1] // tm
    n_col = n // IN_TN
    kv_col = lambda seg: (lambda i, j: (i, jnp.clip(j - seg * IN_COLS_PER_SEG, 0, IN_COLS_PER_SEG - 1)))
    table_spec = pl.BlockSpec((None, tm, HEAD_DIM), lambda i, j: (_table_index(j), i % t_blocks, 0))
    return pl.pallas_call(
        _in_proj_kernel,
        grid=(m // tm, n_col),
        in_specs=[
            pl.BlockSpec((tm, d), lambda i, j: (i, 0)),
            pl.BlockSpec((1, d), lambda i, j: (0, 0)),
            pl.BlockSpec((d, IN_TN), lambda i, j: (0, j)),
            table_spec,
            table_spec,
        ],
        out_specs=[
            pl.BlockSpec((tm, IN_TN), lambda i, j: (i, j)),
            pl.BlockSpec((tm, IN_TN), kv_col(1)),
            pl.BlockSpec((tm, IN_TN), kv_col(2)),
        ],
        out_shape=[
            jax.ShapeDtypeStruct((m, n), BF16),
            jax.ShapeDtypeStruct((m, GROUP_WIDTH), F32),
            jax.ShapeDtypeStruct((m, GROUP_WIDTH), F32),
        ],
        scratch_shapes=[pltpu.VMEM((tm, d), BF16)],
        compiler_params=_params(2),
        name="in_proj",
    )(x2d, g_pre, w_bf16, cos_t, sin_t)


SB_T = 256
SB_HEADS_PER_STEP = 8
EXP_UNDERFLOW = -104.0


def _sb_prompt_kernel(q_ref, k_ref, v_ref, gate_ref, g_ref, o_ref, vt_ref):
    i = pl.program_id(2)
    n_kb = vt_ref.shape[0]
    heads = range(SB_HEADS_PER_STEP)
    lanes = lambda hh: slice(hh * HEAD_DIM, (hh + 1) * HEAD_DIM)

    @pl.when(i == 0)
    def _():
        for kb in range(n_kb):
            vt_ref[kb] = v_ref[kb * SB_T:(kb + 1) * SB_T, :].astype(F32).T.astype(BF16)

    row = lax.broadcasted_iota(jnp.int32, (SB_T, SB_T), 0)
    col = lax.broadcasted_iota(jnp.int32, (SB_T, SB_T), 1)
    later_keys = (col > row).astype(BF16)
    strict = row < col

    def tiles(kb, carries, accs, diagonal):
        start = pl.multiple_of(kb * SB_T, SB_T)
        zs = [lax.dot_general(k_ref[pl.ds(start, SB_T), lanes(hh)], q_ref[:, lanes(hh)], _NT,
                              preferred_element_type=F32) for hh in heads]
        log_betas, lks = [], []
        for z in zs:
            log_beta, log_keep = _log_sigmoids(z.astype(BF16))
            if diagonal:
                log_keep = jnp.where(strict, log_keep, jnp.zeros_like(log_keep))
            log_betas.append(log_beta)
            lks.append(log_keep)
        laters = [jnp.dot(later_keys, lk, preferred_element_type=F32) for lk in lks]
        probs = []
        for hh in heads:
            a = jnp.exp(log_betas[hh] + laters[hh] + carries[hh])
            if diagonal:
                a = jnp.where(strict, a, 0.0)
            probs.append(a.astype(BF16))
        accs = [accs[hh] + jnp.dot(vt_ref[kb, lanes(hh), :], probs[hh], preferred_element_type=F32)
                for hh in heads]
        carries = [carries[hh] + laters[hh][0:1, :] + lks[hh][0:1, :].astype(F32) for hh in heads]
        return carries, accs

    def alive(carries):
        return functools.reduce(jnp.maximum, [jnp.max(c) for c in carries]) > EXP_UNDERFLOW

    carries, accs = tiles(i, [jnp.zeros((1, SB_T), F32) for _ in heads],
                          [jnp.zeros((HEAD_DIM, SB_T), F32) for _ in heads], True)

    def cond(c):
        return (c[0] < i) & c[1]

    def body(c):
        s, _, carries, accs = c
        carries, accs = tiles(i - 1 - s, carries, accs, False)
        return s + 1, alive(carries), carries, accs

    _, _, _, accs = lax.while_loop(cond, body, (jnp.int32(0), alive(carries), carries, accs))
    for hh in heads:
        o = accs[hh].T
        o_ref[:, lanes(hh)] = _head_norm_gate(o, g_ref[:, lanes(hh)], gate_ref[:, lanes(hh)]).astype(BF16)


def _sb_prompt(p3, g_sb):
    b, t, _ = p3.shape
    w = SB_HEADS_PER_STEP * HEAD_DIM
    sb = GROUP_WIDTH // w
    return pl.pallas_call(
        _sb_prompt_kernel,
        grid=(b, sb, t // SB_T),
        in_specs=[
            pl.BlockSpec((None, SB_T, w), lambda b_, h, i: (b_, i, h)),
            pl.BlockSpec((None, t, w), lambda b_, h, i: (b_, 0, sb + h)),
            pl.BlockSpec((None, t, w), lambda b_, h, i: (b_, 0, 2 * sb + h)),
            pl.BlockSpec((None, SB_T, w), lambda b_, h, i: (b_, i, 3 * sb + h)),
            pl.BlockSpec((1, w), lambda b_, h, i: (0, h)),
        ],
        out_specs=pl.BlockSpec((None, SB_T, w), lambda b_, h, i: (b_, i, h)),
        out_shape=jax.ShapeDtypeStruct((b, t, GROUP_WIDTH), BF16),
        scratch_shapes=[pltpu.VMEM((t // SB_T, w, SB_T), BF16)],
        compiler_params=_params(3),
        name="sb_prompt",
    )(p3, p3, p3, p3, g_sb)


RET_L = 256


RET_HEADS_PER_STEP = 4


def _ret_prompt_kernel(lg_ref, q_ref, k_ref, v_ref, gate_ref, g_ref, o_ref, s_ref,
                       w_ref, read_ref, write_ref):
    group = pl.program_id(0)
    t = q_ref.shape[0]
    heads = range(RET_HEADS_PER_STEP)
    lanes = lambda hh: slice(hh * HEAD_DIM, (hh + 1) * HEAD_DIM)
    log_gamma = [lg_ref[group * RET_HEADS_PER_STEP + hh] for hh in heads]

    @pl.when(pl.program_id(1) == 0)
    def _():
        ii = lax.broadcasted_iota(jnp.int32, (RET_L, RET_L), 0)
        jj = lax.broadcasted_iota(jnp.int32, (RET_L, RET_L), 1)
        dist = jnp.abs(ii - jj).astype(F32)
        visible = jj // CHUNK <= ii // CHUNK
        idx = lax.broadcasted_iota(jnp.int32, (RET_L, HEAD_DIM), 0).astype(F32)
        for hh in heads:
            w_ref[hh] = jnp.where(visible, jnp.exp(log_gamma[hh] * dist), 0.0)
            read_ref[hh] = jnp.exp(log_gamma[hh] * (idx + 1.0))
            write_ref[hh] = jnp.exp(log_gamma[hh] * (RET_L - 1.0 - idx))

    decay_block = [jnp.exp(jnp.full((1, HEAD_DIM), log_gamma[hh] * RET_L, F32)) for hh in heads]

    def body(blk, states):
        rows = pl.ds(pl.multiple_of(blk * RET_L, RET_L), RET_L)
        qs = [q_ref[rows, lanes(hh)] for hh in heads]
        ks = [k_ref[rows, lanes(hh)] for hh in heads]
        vs = [v_ref[rows, lanes(hh)] for hh in heads]
        scores = [lax.dot_general(qs[hh], ks[hh], _NT, preferred_element_type=F32) for hh in heads]
        reads = [jnp.dot(qs[hh], states[hh].astype(BF16), preferred_element_type=F32) for hh in heads]
        kws = [(ks[hh].astype(F32) * write_ref[hh]).astype(BF16) for hh in heads]
        writes = [lax.dot_general(kws[hh], vs[hh], _TN, preferred_element_type=F32) for hh in heads]
        weighted = [(scores[hh] * w_ref[hh]).astype(BF16) for hh in heads]
        outs = [jnp.dot(weighted[hh], vs[hh], preferred_element_type=F32) + reads[hh] * read_ref[hh]
                for hh in heads]
        for hh in heads:
            o_ref[rows, lanes(hh)] = _head_norm_gate(
                outs[hh], g_ref[:, lanes(hh)], gate_ref[rows, lanes(hh)]).astype(BF16)
        return [states[hh] * decay_block[hh] + writes[hh] for hh in heads]

    states = lax.fori_loop(0, t // RET_L, body, [jnp.zeros((HEAD_DIM, HEAD_DIM), F32) for _ in heads])
    for hh in heads:
        s_ref[hh] = states[hh]


def _ret_prompt(log_gamma, p3, g_r):
    b, t, _ = p3.shape
    w = RET_HEADS_PER_STEP * HEAD_DIM
    groups = GROUP_WIDTH // w
    col = lambda seg: (lambda g_, b_, lg: (b_, 0, seg * groups + g_))
    return pl.pallas_call(
        _ret_prompt_kernel,
        grid_spec=pltpu.PrefetchScalarGridSpec(
            num_scalar_prefetch=1,
            grid=(groups, b),
            in_specs=[
                pl.BlockSpec((None, t, w), col(4)),
                pl.BlockSpec((None, t, w), col(5)),
                pl.BlockSpec((None, t, w), col(6)),
                pl.BlockSpec((None, t, w), col(7)),
                pl.BlockSpec((1, w), lambda g_, b_, lg: (0, g_)),
            ],
            out_specs=[
                pl.BlockSpec((None, t, w), lambda g_, b_, lg: (b_, 0, g_)),
                pl.BlockSpec((None, RET_HEADS_PER_STEP, HEAD_DIM, HEAD_DIM),
                             lambda g_, b_, lg: (b_, g_, 0, 0)),
            ],
            scratch_shapes=[
                pltpu.VMEM((RET_HEADS_PER_STEP, RET_L, RET_L), F32),
                pltpu.VMEM((RET_HEADS_PER_STEP, RET_L, HEAD_DIM), F32),
                pltpu.VMEM((RET_HEADS_PER_STEP, RET_L, HEAD_DIM), F32),
            ],
        ),
        out_shape=[
            jax.ShapeDtypeStruct((b, t, GROUP_WIDTH), BF16),
            jax.ShapeDtypeStruct((b, N_HEADS, HEAD_DIM, HEAD_DIM), F32),
        ],
        compiler_params=_params(2),
        name="ret_prompt",
    )(log_gamma, p3, p3, p3, p3, g_r)


SBS_TK = 256
SBS_PREFETCH_SLOTS = 2


def _sb_sample_kernel(q_ref, kn_ref, vn_ref, gate_ref, g_ref, kc_hbm, vc_hbm, o_ref,
                      qbd_ref, kbuf, vbuf, sems):
    b = pl.program_id(0)
    n = q_ref.shape[0]
    lanes = N_HEADS * n
    block_rows = SBS_TK * N_HEADS
    n_blocks = kc_hbm.shape[1] // block_rows
    spare = SBS_PREFETCH_SLOTS

    def copies(stream, blk, slot):
        rows = pl.ds((n_blocks - 1 - blk) * block_rows, block_rows)
        return (pltpu.make_async_copy(kc_hbm.at[stream, rows], kbuf.at[slot], sems.at[0, slot]),
                pltpu.make_async_copy(vc_hbm.at[stream, rows], vbuf.at[slot], sems.at[1, slot]))

    def start(pair):
        pair[0].start()
        pair[1].start()

    def wait(pair):
        pair[0].wait()
        pair[1].wait()

    slot = b % SBS_PREFETCH_SLOTS

    @pl.when(b == 0)
    def _():
        start(copies(0, 0, 0))

    @pl.when(b + 1 < pl.num_programs(0))
    def _():
        start(copies(b + 1, 0, (b + 1) % SBS_PREFETCH_SLOTS))

    q = q_ref[...]
    qt = jnp.concatenate([q] * N_HEADS, axis=0)
    rr = lax.broadcasted_iota(jnp.int32, (lanes, GROUP_WIDTH), 0) // n
    cc = lax.broadcasted_iota(jnp.int32, (lanes, GROUP_WIDTH), 1) // HEAD_DIM
    qbd_ref[...] = jnp.where(rr == cc, qt, jnp.zeros_like(qt))

    z = lax.dot_general(kn_ref[...], qbd_ref[...], _NT, preferred_element_type=F32)
    ki = lax.broadcasted_iota(jnp.int32, (n, lanes), 0)
    qi = lax.broadcasted_iota(jnp.int32, (n, lanes), 1) % n
    strict = ki < qi
    log_beta, log_keep = _log_sigmoids(z)
    lk = jnp.where(strict, log_keep, 0.0)
    mj = lax.broadcasted_iota(jnp.int32, (n, n), 0)
    mm = lax.broadcasted_iota(jnp.int32, (n, n), 1)
    later = jnp.dot((mm > mj).astype(F32), lk, preferred_element_type=F32)
    a = jnp.where(strict, jnp.exp(log_beta + later), 0.0)
    acc = lax.dot_general(a.astype(BF16), vn_ref[...], _TN, preferred_element_type=F32)
    carry = jnp.sum(lk, axis=0, keepdims=True)

    kj = lax.broadcasted_iota(jnp.int32, (SBS_TK, SBS_TK), 0)
    km = lax.broadcasted_iota(jnp.int32, (SBS_TK, SBS_TK), 1)
    later_keys = (km > kj).astype(BF16)

    def cached_block(slot, carry, acc):
        def all_heads(buf):
            return jnp.concatenate(
                [buf[slot, pl.ds(hh, SBS_TK, stride=N_HEADS), :].astype(BF16) for hh in range(N_HEADS)],
                axis=1)

        z = lax.dot_general(all_heads(kbuf), qbd_ref[...], _NT, preferred_element_type=F32)
        log_beta, log_keep = _log_sigmoids(z)
        lk = log_keep.astype(BF16)
        later = jnp.dot(later_keys, lk, preferred_element_type=F32)
        a = jnp.exp(log_beta + later + carry)
        acc = acc + lax.dot_general(a.astype(BF16), all_heads(vbuf), _TN, preferred_element_type=F32)
        return carry + later[0:1, :] + lk[0:1, :].astype(F32), acc

    wait(copies(b, 0, slot))
    carry, acc = cached_block(slot, carry, acc)

    def cond(c):
        return (c[0] < n_blocks) & c[1]

    def body(c):
        blk, _, carry, acc = c
        pair = copies(b, blk, spare)
        start(pair)
        wait(pair)
        carry, acc = cached_block(spare, carry, acc)
        return blk + 1, jnp.max(carry) > EXP_UNDERFLOW, carry, acc

    _, _, _, acc = lax.while_loop(
        cond, body, (jnp.int32(1), jnp.max(carry) > EXP_UNDERFLOW, carry, acc))

    for hh in range(N_HEADS):
        cols = slice(hh * HEAD_DIM, (hh + 1) * HEAD_DIM)
        o = acc[hh * n:(hh + 1) * n, cols]
        o_ref[:, cols] = _head_norm_gate(o, g_ref[:, cols], gate_ref[:, cols]).astype(BF16)


def _sb_sample(p3, g_sb, cache_k, cache_v):
    b, n, _ = p3.shape
    seg = lambda sidx: pl.BlockSpec((None, n, GROUP_WIDTH), lambda b_: (b_, 0, sidx))
    slots = SBS_PREFETCH_SLOTS + 1
    return pl.pallas_call(
        _sb_sample_kernel,
        grid=(b,),
        in_specs=[seg(0), seg(1), seg(2), seg(3),
                  pl.BlockSpec((1, GROUP_WIDTH), lambda b_: (0, 0)),
                  pl.BlockSpec(memory_space=pl.ANY),
                  pl.BlockSpec(memory_space=pl.ANY)],
        out_specs=pl.BlockSpec((None, n, GROUP_WIDTH), lambda b_: (b_, 0, 0)),
        out_shape=jax.ShapeDtypeStruct((b, n, GROUP_WIDTH), BF16),
        scratch_shapes=[
            pltpu.VMEM((N_HEADS * n, GROUP_WIDTH), BF16),
            pltpu.VMEM((slots, SBS_TK * N_HEADS, HEAD_DIM), F32),
            pltpu.VMEM((slots, SBS_TK * N_HEADS, HEAD_DIM), F32),
            pltpu.SemaphoreType.DMA((2, slots)),
        ],
        compiler_params=_params(1),
        name="sb_sample",
    )(p3, p3, p3, p3, g_sb, cache_k, cache_v)


def _ret_sample_kernel(lg_ref, q_ref, k_ref, v_ref, gate_ref, g_ref, s_ref, o_ref, so_ref):
    n = q_ref.shape[0]
    ii = lax.broadcasted_iota(jnp.int32, (n, n), 0)
    jj = lax.broadcasted_iota(jnp.int32, (n, n), 1)
    dist = jnp.abs(ii - jj).astype(F32)
    idx = lax.broadcasted_iota(jnp.int32, (n, 1), 0).astype(F32)
    for hh in range(N_HEADS):
        lg = lg_ref[hh]
        cols = slice(hh * HEAD_DIM, (hh + 1) * HEAD_DIM)
        qh, kh, vh = q_ref[:, cols], k_ref[:, cols], v_ref[:, cols]
        s = s_ref[hh]
        scores = lax.dot_general(qh, kh, _NT, preferred_element_type=F32) * jnp.exp(lg * dist)
        o = jnp.dot(scores.astype(BF16), vh, preferred_element_type=F32)
        o = o + jnp.dot(qh, s.astype(BF16), preferred_element_type=F32) * jnp.exp(lg * (idx + 1.0))
        kw = (kh.astype(F32) * jnp.exp(lg * (n - 1.0 - idx))).astype(BF16)
        decay_all = jnp.exp(jnp.full((1, HEAD_DIM), lg * n, F32))
        so_ref[hh] = s * decay_all + lax.dot_general(kw, vh, _TN, preferred_element_type=F32)
        o_ref[:, cols] = _head_norm_gate(o, g_ref[:, cols], gate_ref[:, cols]).astype(BF16)


def _ret_sample(log_gamma, p3, g_r, state):
    b, n, _ = p3.shape
    seg = lambda sidx: pl.BlockSpec((None, n, GROUP_WIDTH), lambda b_, lg: (b_, 0, sidx))
    state_spec = pl.BlockSpec((None, N_HEADS, HEAD_DIM, HEAD_DIM), lambda b_, lg: (b_, 0, 0, 0))
    return pl.pallas_call(
        _ret_sample_kernel,
        grid_spec=pltpu.PrefetchScalarGridSpec(
            num_scalar_prefetch=1,
            grid=(b,),
            in_specs=[seg(4), seg(5), seg(6), seg(7),
                      pl.BlockSpec((1, GROUP_WIDTH), lambda b_, lg: (0, 0)),
                      state_spec],
            out_specs=[pl.BlockSpec((None, n, GROUP_WIDTH), lambda b_, lg: (b_, 0, 0)), state_spec],
        ),
        out_shape=[
            jax.ShapeDtypeStruct((b, n, GROUP_WIDTH), BF16),
            jax.ShapeDtypeStruct(state.shape, F32),
        ],
        compiler_params=_params(1),
        name="ret_sample",
    )(log_gamma, p3, p3, p3, p3, g_r, state)


OUT_TM = 512


def _out_proj_kernel(ysb_ref, yr_ref, w_ref, g_ref, x_ref, o_ref):
    out = jnp.dot(ysb_ref[...], w_ref[0:GROUP_WIDTH, :], preferred_element_type=F32)
    out = out + jnp.dot(yr_ref[...], w_ref[GROUP_WIDTH:2 * GROUP_WIDTH, :], preferred_element_type=F32)
    r = lax.rsqrt(jnp.mean(out * out, axis=-1, keepdims=True) + EPS)
    o_ref[...] = x_ref[...] + out * r * g_ref[...]


def _out_proj(y_sb, y_r, w_bf16, g_post, x2d):
    m, d = x2d.shape
    tm = min(OUT_TM, m)
    row = lambda width: pl.BlockSpec((tm, width), lambda i: (i, 0))
    return pl.pallas_call(
        _out_proj_kernel,
        grid=(m // tm,),
        in_specs=[row(GROUP_WIDTH), row(GROUP_WIDTH),
                  pl.BlockSpec(w_bf16.shape, lambda i: (0, 0)),
                  pl.BlockSpec((1, d), lambda i: (0, 0)),
                  row(d)],
        out_specs=row(d),
        out_shape=jax.ShapeDtypeStruct((m, d), F32),
        compiler_params=_params(1),
        name="out_proj",
    )(y_sb, y_r, w_bf16, g_post, x2d)


def _epilogue_tables(positions):
    half = HEAD_DIM // 2
    inv = ROPE_BASE ** (-np.arange(half, dtype=np.float64) / half)
    ang = positions.astype(np.float64)[:, None] * inv[None, :]
    cos = np.concatenate([np.cos(ang), np.cos(ang)], axis=-1)
    sin = np.concatenate([-np.sin(ang), np.sin(ang)], axis=-1)
    one, zero = np.ones_like(cos), np.zeros_like(sin)
    cos4 = np.stack([one * Q_SCALE, one, cos, cos * Q_SCALE])
    sin4 = np.stack([zero, zero, sin, sin * Q_SCALE])
    return jnp.asarray(cos4, F32), jnp.asarray(sin4, F32)


IN_TM_PROMPT = 1024


def kernel(x_prompt, x_sample, cache_sb_k, cache_sb_v, state_ret, norm_pre, w_in, sb_head_norm,
           ret_head_norm, w_out, norm_post):
    depth = norm_pre.shape[0]
    assert depth == 1
    b_p, t_p, d = x_prompt.shape
    b_s, t_s, _ = x_sample.shape
    past = cache_sb_k.shape[2]
    log_gamma = jnp.asarray(np.log(1.0 - 2.0 ** (-5.0 - np.arange(N_HEADS))), F32)

    l = 0
    w_in_b = w_in[l].astype(BF16)
    w_out_b = w_out[l].astype(BF16)
    g_pre = norm_pre[l][None, :]
    g_post = norm_post[l][None, :]
    g_sb = sb_head_norm[l][None, :]
    g_r = ret_head_norm[l][None, :]

    cos_p, sin_p = _epilogue_tables(np.arange(t_p))
    xp2 = x_prompt.reshape(b_p * t_p, d)
    p_p, k_p, v_p = _in_proj(xp2, g_pre, w_in_b, cos_p, sin_p, IN_TM_PROMPT)
    p3 = p_p.reshape(b_p, t_p, N_SEGMENTS * GROUP_WIDTH)
    y_sb = _sb_prompt(p3, g_sb)
    y_r, s_p = _ret_prompt(log_gamma, p3, g_r)
    y_p = _out_proj(y_sb.reshape(b_p * t_p, GROUP_WIDTH), y_r.reshape(b_p * t_p, GROUP_WIDTH),
                    w_out_b, g_post, xp2).reshape(b_p, t_p, d)

    cos_s, sin_s = _epilogue_tables(np.tile(past + np.arange(t_s), b_s))
    xs2 = x_sample.reshape(b_s * t_s, d)
    p_s, k_s, v_s = _in_proj(xs2, g_pre, w_in_b, cos_s, sin_s, b_s * t_s)
    ps3 = p_s.reshape(b_s, t_s, N_SEGMENTS * GROUP_WIDTH)
    ys_sb = _sb_sample(ps3, g_sb, cache_sb_k[l].reshape(b_s, past * N_HEADS, HEAD_DIM),
                       cache_sb_v[l].reshape(b_s, past * N_HEADS, HEAD_DIM))
    ys_r, s_s = _ret_sample(log_gamma, ps3, g_r, state_ret[l])
    y_s = _out_proj(ys_sb.reshape(b_s * t_s, GROUP_WIDTH), ys_r.reshape(b_s * t_s, GROUP_WIDTH),
                    w_out_b, g_post, xs2).reshape(b_s, t_s, d)

    heads = lambda a, bb, tt: a.reshape(1, bb, tt, N_HEADS, HEAD_DIM)
    return (y_p, y_s, heads(k_p, b_p, t_p), heads(v_p, b_p, t_p), s_p[None],
            heads(k_s, b_s, t_s), heads(v_s, b_s, t_s), s_s[None])
```

```python
import functools

import numpy as np
import jax
import jax.numpy as jnp
from jax import lax
from jax.experimental import pallas as pl
from jax.experimental.pallas import tpu as pltpu

HEAD_DIM = 128
N_HEADS = 8
GROUP_WIDTH = N_HEADS * HEAD_DIM
N_SEGMENTS = 8
CHUNK = 64
ROPE_BASE = 10000.0
EPS = 1e-6
Q_SCALE = HEAD_DIM ** -0.5

F32 = jnp.float32
BF16 = jnp.bfloat16

_NT = (((1,), (1,)), ((), ()))
_TN = (((0,), (0,)), ((), ()))

VMEM_LIMIT = 56 * 1024 * 1024


def _params(n_axes):
    return pltpu.CompilerParams(
        dimension_semantics=("arbitrary",) * n_axes, vmem_limit_bytes=VMEM_LIMIT)


def _log_sigmoids(z):
    l = jnp.log(1.0 + jnp.exp(-jnp.abs(z)))
    log_beta = jnp.minimum(z, 0.0) - l
    return log_beta, log_beta - z


def _head_norm_gate(o, g, gate):
    y = o * lax.rsqrt(jnp.mean(o * o, axis=-1, keepdims=True) + EPS) * g
    gf = gate.astype(F32)
    return y * (gf * jax.nn.sigmoid(gf))


IN_TM = 256
IN_TN = 512
_SCALE_Q, _IDENTITY, _ROTARY, _ROTARY_SCALED = range(4)
_TABLE_OF_SEGMENT = (_SCALE_Q, _IDENTITY, _IDENTITY, _IDENTITY, _ROTARY, _ROTARY_SCALED, _IDENTITY, _IDENTITY)
_K_SEGMENT, _V_SEGMENT = 1, 2


def _in_proj_kernel(x_ref, g_ref, w_ref, cos_ref, sin_ref, p_ref, k_ref, v_ref):
    x = x_ref[...]
    r = lax.rsqrt(jnp.mean(x * x, axis=-1, keepdims=True) + EPS)
    h = (x * r * g_ref[...]).astype(BF16)
    for seg, table in enumerate(_TABLE_OF_SEGMENT):
        for c0 in range(0, GROUP_WIDTH, IN_TN):
            col0 = seg * GROUP_WIDTH + c0
            acc = jnp.dot(h, w_ref[:, col0:col0 + IN_TN], preferred_element_type=F32)
            for hh in range(IN_TN // HEAD_DIM):
                xs = acc[:, hh * HEAD_DIM:(hh + 1) * HEAD_DIM]
                y = xs * cos_ref[table]
                if table in (_ROTARY, _ROTARY_SCALED):
                    y = y + pltpu.roll(xs, HEAD_DIM // 2, 1) * sin_ref[table]
                p_ref[:, col0 + hh * HEAD_DIM:col0 + (hh + 1) * HEAD_DIM] = y.astype(BF16)
            if seg == _K_SEGMENT:
                k_ref[:, c0:c0 + IN_TN] = acc
            if seg == _V_SEGMENT:
                v_ref[:, c0:c0 + IN_TN] = acc


def _in_proj(x2d, g_pre, w_bf16, cos_t, sin_t):
    m, d = x2d.shape
    n = w_bf16.shape[1]
    t_blocks = cos_t.shape[1] // IN_TM
    table_spec = pl.BlockSpec((len(cos_t), IN_TM, HEAD_DIM), lambda i: (0, i % t_blocks, 0))
    return pl.pallas_call(
        _in_proj_kernel,
        grid=(m // IN_TM,),
        in_specs=[
            pl.BlockSpec((IN_TM, d), lambda i: (i, 0)),
            pl.BlockSpec((1, d), lambda i: (0, 0)),
            pl.BlockSpec((d, n), lambda i: (0, 0), pipeline_mode=pl.Buffered(1)),
            table_spec,
            table_spec,
        ],
        out_specs=[
            pl.BlockSpec((IN_TM, n), lambda i: (i, 0)),
            pl.BlockSpec((IN_TM, GROUP_WIDTH), lambda i: (i, 0)),
            pl.BlockSpec((IN_TM, GROUP_WIDTH), lambda i: (i, 0)),
        ],
        out_shape=[
            jax.ShapeDtypeStruct((m, n), BF16),
            jax.ShapeDtypeStruct((m, GROUP_WIDTH), F32),
            jax.ShapeDtypeStruct((m, GROUP_WIDTH), F32),
        ],
        compiler_params=_params(1),
        name="in_proj",
    )(x2d, g_pre, w_bf16, cos_t, sin_t)


SB_T = 256
SB_HEADS_PER_STEP = 8
EXP_UNDERFLOW = -104.0


def _sb_prompt_kernel(q_ref, k_ref, v_ref, gate_ref, g_ref, o_ref, vt_ref):
    i = pl.program_id(2)
    n_kb = vt_ref.shape[0]
    heads = range(SB_HEADS_PER_STEP)
    lanes = lambda hh: slice(hh * HEAD_DIM, (hh + 1) * HEAD_DIM)

    @pl.when(i == 0)
    def _():
        for kb in range(n_kb):
            vt_ref[kb] = v_ref[kb * SB_T:(kb + 1) * SB_T, :].astype(F32).T.astype(BF16)

    row = lax.broadcasted_iota(jnp.int32, (SB_T, SB_T), 0)
    col = lax.broadcasted_iota(jnp.int32, (SB_T, SB_T), 1)
    minus_from = -(col >= row).astype(BF16)
    strict = row < col

    def tiles(kb, carries, accs, diagonal):
        start = pl.multiple_of(kb * SB_T, SB_T)
        zs = [lax.dot_general(k_ref[pl.ds(start, SB_T), lanes(hh)], q_ref[:, lanes(hh)], _NT,
                              preferred_element_type=F32) for hh in heads]
        drops = []
        for z in zs:
            zb = z.astype(BF16)
            drop = jnp.maximum(zb, 0.0) + jnp.log(1.0 + jnp.exp(-jnp.abs(zb)))
            if diagonal:
                drop = jnp.where(strict, drop, jnp.zeros_like(drop))
            drops.append(drop)
        tails = [jnp.dot(minus_from, drop, preferred_element_type=F32) for drop in drops]
        probs = []
        for hh in heads:
            a = jnp.exp(zs[hh] + tails[hh] + carries[hh])
            if diagonal:
                a = jnp.where(strict, a, 0.0)
            probs.append(a.astype(BF16))
        accs = [accs[hh] + jnp.dot(vt_ref[kb, lanes(hh), :], probs[hh], preferred_element_type=F32)
                for hh in heads]
        carries = [carries[hh] + tails[hh][0:1, :] for hh in heads]
        return carries, accs

    def alive(carries):
        return functools.reduce(jnp.maximum, [jnp.max(c) for c in carries]) > EXP_UNDERFLOW

    carries, accs = tiles(i, [jnp.zeros((1, SB_T), F32) for _ in heads],
                          [jnp.zeros((HEAD_DIM, SB_T), F32) for _ in heads], True)

    def cond(c):
        return (c[0] < i) & c[1]

    def body(c):
        s, _, carries, accs = c
        carries, accs = tiles(i - 1 - s, carries, accs, False)
        return s + 1, alive(carries), carries, accs

    _, _, _, accs = lax.while_loop(cond, body, (jnp.int32(0), alive(carries), carries, accs))
    for hh in heads:
        acc = accs[hh]
        normed = (acc * lax.rsqrt(jnp.mean(acc * acc, axis=0, keepdims=True) + EPS)).T
        gate = gate_ref[:, lanes(hh)].astype(F32)
        o_ref[:, lanes(hh)] = (normed * g_ref[:, lanes(hh)] * (gate * jax.nn.sigmoid(gate))).astype(BF16)


def _sb_prompt(p3, g_sb):
    b, t, _ = p3.shape
    w = SB_HEADS_PER_STEP * HEAD_DIM
    sb = GROUP_WIDTH // w
    return pl.pallas_call(
        _sb_prompt_kernel,
        grid=(b, sb, t // SB_T),
        in_specs=[
            pl.BlockSpec((None, SB_T, w), lambda b_, h, i: (b_, i, h)),
            pl.BlockSpec((None, t, w), lambda b_, h, i: (b_, 0, sb + h)),
            pl.BlockSpec((None, t, w), lambda b_, h, i: (b_, 0, 2 * sb + h)),
            pl.BlockSpec((None, SB_T, w), lambda b_, h, i: (b_, i, 3 * sb + h)),
            pl.BlockSpec((1, w), lambda b_, h, i: (0, h)),
        ],
        out_specs=pl.BlockSpec((None, SB_T, w), lambda b_, h, i: (b_, i, h)),
        out_shape=jax.ShapeDtypeStruct((b, t, GROUP_WIDTH), BF16),
        scratch_shapes=[pltpu.VMEM((t // SB_T, w, SB_T), BF16)],
        compiler_params=_params(3),
        name="sb_prompt",
    )(p3, p3, p3, p3, g_sb)


RET_L = 256


RET_HEADS_PER_STEP = 4


def _ret_prompt_kernel(lg_ref, q_ref, k_ref, v_ref, gate_ref, g_ref, o_ref, s_ref,
                       w_ref, read_ref, write_ref):
    group = pl.program_id(0)
    t = q_ref.shape[0]
    heads = range(RET_HEADS_PER_STEP)
    lanes = lambda hh: slice(hh * HEAD_DIM, (hh + 1) * HEAD_DIM)
    log_gamma = [lg_ref[group * RET_HEADS_PER_STEP + hh] for hh in heads]

    @pl.when(pl.program_id(1) == 0)
    def _():
        ii = lax.broadcasted_iota(jnp.int32, (RET_L, RET_L), 0)
        jj = lax.broadcasted_iota(jnp.int32, (RET_L, RET_L), 1)
        dist = jnp.abs(ii - jj).astype(F32)
        visible = jj // CHUNK <= ii // CHUNK
        idx = lax.broadcasted_iota(jnp.int32, (RET_L, HEAD_DIM), 0).astype(F32)
        for hh in heads:
            w_ref[hh] = jnp.where(visible, jnp.exp(log_gamma[hh] * dist), 0.0)
            read_ref[hh] = jnp.exp(log_gamma[hh] * (idx + 1.0))
            write_ref[hh] = jnp.exp(log_gamma[hh] * (RET_L - 1.0 - idx))

    decay_block = [jnp.exp(jnp.full((1, HEAD_DIM), log_gamma[hh] * RET_L, F32)) for hh in heads]

    def body(blk, states):
        rows = pl.ds(pl.multiple_of(blk * RET_L, RET_L), RET_L)
        qs = [q_ref[rows, lanes(hh)] for hh in heads]
        ks = [k_ref[rows, lanes(hh)] for hh in heads]
        vs = [v_ref[rows, lanes(hh)] for hh in heads]
        scores = [lax.dot_general(qs[hh], ks[hh], _NT, preferred_element_type=F32) for hh in heads]
        reads = [jnp.dot(qs[hh], states[hh].astype(BF16), preferred_element_type=F32) for hh in heads]
        kws = [(ks[hh].astype(F32) * write_ref[hh]).astype(BF16) for hh in heads]
        writes = [lax.dot_general(kws[hh], vs[hh], _TN, preferred_element_type=F32) for hh in heads]
        weighted = [(scores[hh] * w_ref[hh]).astype(BF16) for hh in heads]
        outs = [jnp.dot(weighted[hh], vs[hh], preferred_element_type=F32) + reads[hh] * read_ref[hh]
                for hh in heads]
        for hh in heads:
            o_ref[rows, lanes(hh)] = _head_norm_gate(
                outs[hh], g_ref[:, lanes(hh)], gate_ref[rows, lanes(hh)]).astype(BF16)
        return [states[hh] * decay_block[hh] + writes[hh] for hh in heads]

    states = lax.fori_loop(0, t // RET_L, body, [jnp.zeros((HEAD_DIM, HEAD_DIM), F32) for _ in heads])
    for hh in heads:
        s_ref[hh] = states[hh]


def _ret_prompt(log_gamma, p3, g_r):
    b, t, _ = p3.shape
    w = RET_HEADS_PER_STEP * HEAD_DIM
    groups = GROUP_WIDTH // w
    col = lambda seg: (lambda g_, b_, lg: (b_, 0, seg * groups + g_))
    return pl.pallas_call(
        _ret_prompt_kernel,
        grid_spec=pltpu.PrefetchScalarGridSpec(
            num_scalar_prefetch=1,
            grid=(groups, b),
            in_specs=[
                pl.BlockSpec((None, t, w), col(4)),
                pl.BlockSpec((None, t, w), col(5)),
                pl.BlockSpec((None, t, w), col(6)),
                pl.BlockSpec((None, t, w), col(7)),
                pl.BlockSpec((1, w), lambda g_, b_, lg: (0, g_)),
            ],
            out_specs=[
                pl.BlockSpec((None, t, w), lambda g_, b_, lg: (b_, 0, g_)),
                pl.BlockSpec((None, RET_HEADS_PER_STEP, HEAD_DIM, HEAD_DIM),
                             lambda g_, b_, lg: (b_, g_, 0, 0)),
            ],
            scratch_shapes=[
                pltpu.VMEM((RET_HEADS_PER_STEP, RET_L, RET_L), F32),
                pltpu.VMEM((RET_HEADS_PER_STEP, RET_L, HEAD_DIM), F32),
                pltpu.VMEM((RET_HEADS_PER_STEP, RET_L, HEAD_DIM), F32),
            ],
        ),
        out_shape=[
            jax.ShapeDtypeStruct((b, t, GROUP_WIDTH), BF16),
            jax.ShapeDtypeStruct((b, N_HEADS, HEAD_DIM, HEAD_DIM), F32),
        ],
        compiler_params=_params(2),
        name="ret_prompt",
    )(log_gamma, p3, p3, p3, p3, g_r)


SBS_TK = 256
SBS_PREFETCH_SLOTS = 2


def _sb_sample_kernel(q_ref, kn_ref, vn_ref, gate_ref, g_ref, kc_hbm, vc_hbm, o_ref,
                      qbd_ref, kbuf, vbuf, sems):
    b = pl.program_id(0)
    n = q_ref.shape[0]
    lanes = N_HEADS * n
    block_rows = SBS_TK * N_HEADS
    n_blocks = kc_hbm.shape[1] // block_rows
    spare = SBS_PREFETCH_SLOTS

    def copies(stream, blk, slot):
        rows = pl.ds((n_blocks - 1 - blk) * block_rows, block_rows)
        return (pltpu.make_async_copy(kc_hbm.at[stream, rows], kbuf.at[slot], sems.at[0, slot]),
                pltpu.make_async_copy(vc_hbm.at[stream, rows], vbuf.at[slot], sems.at[1, slot]))

    def start(pair):
        pair[0].start()
        pair[1].start()

    def wait(pair):
        pair[0].wait()
        pair[1].wait()

    slot = b % SBS_PREFETCH_SLOTS

    @pl.when(b == 0)
    def _():
        start(copies(0, 0, 0))

    @pl.when(b + 1 < pl.num_programs(0))
    def _():
        start(copies(b + 1, 0, (b + 1) % SBS_PREFETCH_SLOTS))

    q = q_ref[...]
    qt = jnp.concatenate([q] * N_HEADS, axis=0)
    rr = lax.broadcasted_iota(jnp.int32, (lanes, GROUP_WIDTH), 0) // n
    cc = lax.broadcasted_iota(jnp.int32, (lanes, GROUP_WIDTH), 1) // HEAD_DIM
    qbd_ref[...] = jnp.where(rr == cc, qt, jnp.zeros_like(qt))

    z = lax.dot_general(kn_ref[...], qbd_ref[...], _NT, preferred_element_type=F32)
    ki = lax.broadcasted_iota(jnp.int32, (n, lanes), 0)
    qi = lax.broadcasted_iota(jnp.int32, (n, lanes), 1) % n
    strict = ki < qi
    log_beta, log_keep = _log_sigmoids(z)
    lk = jnp.where(strict, log_keep, 0.0)
    mj = lax.broadcasted_iota(jnp.int32, (n, n), 0)
    mm = lax.broadcasted_iota(jnp.int32, (n, n), 1)
    later = jnp.dot((mm > mj).astype(F32), lk, preferred_element_type=F32)
    a = jnp.where(strict, jnp.exp(log_beta + later), 0.0)
    acc = lax.dot_general(a.astype(BF16), vn_ref[...], _TN, preferred_element_type=F32)
    carry = jnp.sum(lk, axis=0, keepdims=True)

    kj = lax.broadcasted_iota(jnp.int32, (SBS_TK, SBS_TK), 0)
    km = lax.broadcasted_iota(jnp.int32, (SBS_TK, SBS_TK), 1)
    later_keys = (km > kj).astype(BF16)

    def cached_block(slot, carry, acc):
        def all_heads(buf):
            return jnp.concatenate(
                [buf[slot, pl.ds(hh, SBS_TK, stride=N_HEADS), :].astype(BF16) for hh in range(N_HEADS)],
                axis=1)

        z = lax.dot_general(all_heads(kbuf), qbd_ref[...], _NT, preferred_element_type=F32)
        log_beta, log_keep = _log_sigmoids(z)
        lk = log_keep.astype(BF16)
        later = jnp.dot(later_keys, lk, preferred_element_type=F32)
        a = jnp.exp(log_beta + later + carry)
        acc = acc + lax.dot_general(a.astype(BF16), all_heads(vbuf), _TN, preferred_element_type=F32)
        return carry + later[0:1, :] + lk[0:1, :].astype(F32), acc

    wait(copies(b, 0, slot))
    carry, acc = cached_block(slot, carry, acc)

    def cond(c):
        return (c[0] < n_blocks) & c[1]

    def body(c):
        blk, _, carry, acc = c
        pair = copies(b, blk, spare)
        start(pair)
        wait(pair)
        carry, acc = cached_block(spare, carry, acc)
        return blk + 1, jnp.max(carry) > EXP_UNDERFLOW, carry, acc

    _, _, _, acc = lax.while_loop(
        cond, body, (jnp.int32(1), jnp.max(carry) > EXP_UNDERFLOW, carry, acc))

    for hh in range(N_HEADS):
        cols = slice(hh * HEAD_DIM, (hh + 1) * HEAD_DIM)
        o = acc[hh * n:(hh + 1) * n, cols]
        o_ref[:, cols] = _head_norm_gate(o, g_ref[:, cols], gate_ref[:, cols]).astype(BF16)


def _sb_sample(p3, g_sb, cache_k, cache_v):
    b, n, _ = p3.shape
    seg = lambda sidx: pl.BlockSpec((None, n, GROUP_WIDTH), lambda b_: (b_, 0, sidx))
    slots = SBS_PREFETCH_SLOTS + 1
    return pl.pallas_call(
        _sb_sample_kernel,
        grid=(b,),
        in_specs=[seg(0), seg(1), seg(2), seg(3),
                  pl.BlockSpec((1, GROUP_WIDTH), lambda b_: (0, 0)),
                  pl.BlockSpec(memory_space=pl.ANY),
                  pl.BlockSpec(memory_space=pl.ANY)],
        out_specs=pl.BlockSpec((None, n, GROUP_WIDTH), lambda b_: (b_, 0, 0)),
        out_shape=jax.ShapeDtypeStruct((b, n, GROUP_WIDTH), BF16),
        scratch_shapes=[
            pltpu.VMEM((N_HEADS * n, GROUP_WIDTH), BF16),
            pltpu.VMEM((slots, SBS_TK * N_HEADS, HEAD_DIM), F32),
            pltpu.VMEM((slots, SBS_TK * N_HEADS, HEAD_DIM), F32),
            pltpu.SemaphoreType.DMA((2, slots)),
        ],
        compiler_params=_params(1),
        name="sb_sample",
    )(p3, p3, p3, p3, g_sb, cache_k, cache_v)


def _ret_sample_kernel(lg_ref, q_ref, k_ref, v_ref, gate_ref, g_ref, s_ref, o_ref, so_ref):
    n = q_ref.shape[0]
    ii = lax.broadcasted_iota(jnp.int32, (n, n), 0)
    jj = lax.broadcasted_iota(jnp.int32, (n, n), 1)
    dist = jnp.abs(ii - jj).astype(F32)
    idx = lax.broadcasted_iota(jnp.int32, (n, 1), 0).astype(F32)
    heads = range(N_HEADS)
    cols = [slice(hh * HEAD_DIM, (hh + 1) * HEAD_DIM) for hh in heads]
    lg = [lg_ref[hh] for hh in heads]
    qs = [q_ref[:, c] for c in cols]
    ks = [k_ref[:, c] for c in cols]
    vs = [v_ref[:, c] for c in cols]
    scores = [lax.dot_general(qs[hh], ks[hh], _NT, preferred_element_type=F32) for hh in heads]
    reads = [jnp.dot(qs[hh], s_ref[hh].astype(BF16), preferred_element_type=F32) for hh in heads]
    kws = [(ks[hh].astype(F32) * jnp.exp(lg[hh] * (n - 1.0 - idx))).astype(BF16) for hh in heads]
    writes = [lax.dot_general(kws[hh], vs[hh], _TN, preferred_element_type=F32) for hh in heads]
    weighted = [(scores[hh] * jnp.exp(lg[hh] * dist)).astype(BF16) for hh in heads]
    outs = [jnp.dot(weighted[hh], vs[hh], preferred_element_type=F32)
            + reads[hh] * jnp.exp(lg[hh] * (idx + 1.0)) for hh in heads]
    for hh in heads:
        decay_all = jnp.exp(jnp.full((1, HEAD_DIM), lg[hh] * n, F32))
        so_ref[hh] = s_ref[hh] * decay_all + writes[hh]
        o_ref[:, cols[hh]] = _head_norm_gate(
            outs[hh], g_ref[:, cols[hh]], gate_ref[:, cols[hh]]).astype(BF16)


def _ret_sample(log_gamma, p3, g_r, state):
    b, n, _ = p3.shape
    seg = lambda sidx: pl.BlockSpec((None, n, GROUP_WIDTH), lambda b_, lg: (b_, 0, sidx))
    state_spec = pl.BlockSpec((None, N_HEADS, HEAD_DIM, HEAD_DIM), lambda b_, lg: (b_, 0, 0, 0))
    return pl.pallas_call(
        _ret_sample_kernel,
        grid_spec=pltpu.PrefetchScalarGridSpec(
            num_scalar_prefetch=1,
            grid=(b,),
            in_specs=[seg(4), seg(5), seg(6), seg(7),
                      pl.BlockSpec((1, GROUP_WIDTH), lambda b_, lg: (0, 0)),
                      state_spec],
            out_specs=[pl.BlockSpec((None, n, GROUP_WIDTH), lambda b_, lg: (b_, 0, 0)), state_spec],
        ),
        out_shape=[
            jax.ShapeDtypeStruct((b, n, GROUP_WIDTH), BF16),
            jax.ShapeDtypeStruct(state.shape, F32),
        ],
        compiler_params=_params(1),
        name="ret_sample",
    )(log_gamma, p3, p3, p3, p3, g_r, state)


OUT_TM = 512


def _out_proj_kernel(ysb_ref, yr_ref, w_ref, g_ref, x_ref, o_ref):
    out = jnp.dot(ysb_ref[...], w_ref[0:GROUP_WIDTH, :], preferred_element_type=F32)
    out = out + jnp.dot(yr_ref[...], w_ref[GROUP_WIDTH:2 * GROUP_WIDTH, :], preferred_element_type=F32)
    r = lax.rsqrt(jnp.mean(out * out, axis=-1, keepdims=True) + EPS)
    o_ref[...] = x_ref[...] + out * r * g_ref[...]


def _out_proj(y_sb, y_r, w_bf16, g_post, x2d):
    m, d = x2d.shape
    tm = min(OUT_TM, m)
    row = lambda width: pl.BlockSpec((tm, width), lambda i: (i, 0))
    return pl.pallas_call(
        _out_proj_kernel,
        grid=(m // tm,),
        in_specs=[row(GROUP_WIDTH), row(GROUP_WIDTH),
                  pl.BlockSpec(w_bf16.shape, lambda i: (0, 0)),
                  pl.BlockSpec((1, d), lambda i: (0, 0)),
                  row(d)],
        out_specs=row(d),
        out_shape=jax.ShapeDtypeStruct((m, d), F32),
        compiler_params=_params(1),
        name="out_proj",
    )(y_sb, y_r, w_bf16, g_post, x2d)


def _epilogue_tables(positions):
    half = HEAD_DIM // 2
    inv = ROPE_BASE ** (-np.arange(half, dtype=np.float64) / half)
    ang = positions.astype(np.float64)[:, None] * inv[None, :]
    cos = np.concatenate([np.cos(ang), np.cos(ang)], axis=-1)
    sin = np.concatenate([-np.sin(ang), np.sin(ang)], axis=-1)
    one, zero = np.ones_like(cos), np.zeros_like(sin)
    cos4 = np.stack([one * Q_SCALE, one, cos, cos * Q_SCALE])
    sin4 = np.stack([zero, zero, sin, sin * Q_SCALE])
    return jnp.asarray(cos4, F32), jnp.asarray(sin4, F32)


def kernel(x_prompt, x_sample, cache_sb_k, cache_sb_v, state_ret, norm_pre, w_in, sb_head_norm,
           ret_head_norm, w_out, norm_post):
    depth = norm_pre.shape[0]
    assert depth == 1
    b_p, t_p, d = x_prompt.shape
    b_s, t_s, _ = x_sample.shape
    past = cache_sb_k.shape[2]
    log_gamma = jnp.asarray(np.log(1.0 - 2.0 ** (-5.0 - np.arange(N_HEADS))), F32)

    l = 0
    w_in_b = w_in[l].astype(BF16)
    w_out_b = w_out[l].astype(BF16)
    g_pre = norm_pre[l][None, :]
    g_post = norm_post[l][None, :]
    g_sb = sb_head_norm[l][None, :]
    g_r = ret_head_norm[l][None, :]

    cos_p, sin_p = _epilogue_tables(np.arange(t_p))
    xp2 = x_prompt.reshape(b_p * t_p, d)
    p_p, k_p, v_p = _in_proj(xp2, g_pre, w_in_b, cos_p, sin_p)
    p3 = p_p.reshape(b_p, t_p, N_SEGMENTS * GROUP_WIDTH)
    y_sb = _sb_prompt(p3, g_sb)
    y_r, s_p = _ret_prompt(log_gamma, p3, g_r)
    y_p = _out_proj(y_sb.reshape(b_p * t_p, GROUP_WIDTH), y_r.reshape(b_p * t_p, GROUP_WIDTH),
                    w_out_b, g_post, xp2).reshape(b_p, t_p, d)

    cos_s, sin_s = _epilogue_tables(np.tile(past + np.arange(t_s), b_s))
    xs2 = x_sample.reshape(b_s * t_s, d)
    p_s, k_s, v_s = _in_proj(xs2, g_pre, w_in_b, cos_s, sin_s)
    ps3 = p_s.reshape(b_s, t_s, N_SEGMENTS * GROUP_WIDTH)
    ys_sb = _sb_sample(ps3, g_sb, cache_sb_k[l].reshape(b_s, past * N_HEADS, HEAD_DIM),
                       cache_sb_v[l].reshape(b_s, past * N_HEADS, HEAD_DIM))
    ys_r, s_s = _ret_sample(log_gamma, ps3, g_r, state_ret[l])
    y_s = _out_proj(ys_sb.reshape(b_s * t_s, GROUP_WIDTH), ys_r.reshape(b_s * t_s, GROUP_WIDTH),
                    w_out_b, g_post, xs2).reshape(b_s, t_s, d)

    heads = lambda a, bb, tt: a.reshape(1, bb, tt, N_HEADS, HEAD_DIM)
    return (y_p, y_s, heads(k_p, b_p, t_p), heads(v_p, b_p, t_p), s_p[None],
            heads(k_s, b_s, t_s), heads(v_s, b_s, t_s), s_s[None])
```

```python
import functools

import numpy as np
import jax
import jax.numpy as jnp
from jax import lax
from jax.experimental import pallas as pl
from jax.experimental.pallas import tpu as pltpu

HEAD_DIM = 128
N_HEADS = 8
GROUP_WIDTH = N_HEADS * HEAD_DIM
N_SEGMENTS = 8
CHUNK = 64
ROPE_BASE = 10000.0
EPS = 1e-6
Q_SCALE = HEAD_DIM ** -0.5

F32 = jnp.float32
BF16 = jnp.bfloat16

_NT = (((1,), (1,)), ((), ()))
_TN = (((0,), (0,)), ((), ()))

VMEM_LIMIT = 56 * 1024 * 1024


def _params(n_axes):
    return pltpu.CompilerParams(
        dimension_semantics=("arbitrary",) * n_axes, vmem_limit_bytes=VMEM_LIMIT)


def _log_sigmoids(z):
    l = jnp.log(1.0 + jnp.exp(-jnp.abs(z)))
    log_beta = jnp.minimum(z, 0.0) - l
    return log_beta, log_beta - z


def _head_norm_gate(o, g, gate):
    y = o * lax.rsqrt(jnp.mean(o * o, axis=-1, keepdims=True) + EPS) * g
    gf = gate.astype(F32)
    return y * (gf * jax.nn.sigmoid(gf))


IN_TM = 256
IN_TN = 512
IN_W_PARTS = 4
_SCALE_Q, _IDENTITY, _ROTARY, _ROTARY_SCALED = range(4)
_TABLE_OF_SEGMENT = (_SCALE_Q, _IDENTITY, _IDENTITY, _IDENTITY, _ROTARY, _ROTARY_SCALED, _IDENTITY, _IDENTITY)
_K_SEGMENT, _V_SEGMENT = 1, 2


def _in_proj_kernel(x_ref, g_ref, *refs):
    w_refs = refs[:IN_W_PARTS]
    cos_ref, sin_ref, p_ref, k_ref, v_ref = refs[IN_W_PARTS:]
    part_width = w_refs[0].shape[1]
    x = x_ref[...]
    r = lax.rsqrt(jnp.mean(x * x, axis=-1, keepdims=True) + EPS)
    h = (x * r * g_ref[...]).astype(BF16)
    for seg, table in enumerate(_TABLE_OF_SEGMENT):
        for c0 in range(0, GROUP_WIDTH, IN_TN):
            col0 = seg * GROUP_WIDTH + c0
            w_ref, w_col = w_refs[col0 // part_width], col0 % part_width
            acc = jnp.dot(h, w_ref[:, w_col:w_col + IN_TN], preferred_element_type=F32)
            for hh in range(IN_TN // HEAD_DIM):
                xs = acc[:, hh * HEAD_DIM:(hh + 1) * HEAD_DIM]
                y = xs * cos_ref[table]
                if table in (_ROTARY, _ROTARY_SCALED):
                    y = y + pltpu.roll(xs, HEAD_DIM // 2, 1) * sin_ref[table]
                p_ref[:, col0 + hh * HEAD_DIM:col0 + (hh + 1) * HEAD_DIM] = y.astype(BF16)
            if seg == _K_SEGMENT:
                k_ref[:, c0:c0 + IN_TN] = acc
            if seg == _V_SEGMENT:
                v_ref[:, c0:c0 + IN_TN] = acc


def _in_proj(x2d, g_pre, w_bf16, cos_t, sin_t):
    m, d = x2d.shape
    n = w_bf16.shape[1]
    t_blocks = cos_t.shape[1] // IN_TM
    table_spec = pl.BlockSpec((len(cos_t), IN_TM, HEAD_DIM), lambda i: (0, i % t_blocks, 0))
    w_part = lambda c: pl.BlockSpec((d, n // IN_W_PARTS), lambda i: (0, c), pipeline_mode=pl.Buffered(1))
    return pl.pallas_call(
        _in_proj_kernel,
        grid=(m // IN_TM,),
        in_specs=[
            pl.BlockSpec((IN_TM, d), lambda i: (i, 0)),
            pl.BlockSpec((1, d), lambda i: (0, 0)),
            *[w_part(c) for c in range(IN_W_PARTS)],
            table_spec,
            table_spec,
        ],
        out_specs=[
            pl.BlockSpec((IN_TM, n), lambda i: (i, 0)),
            pl.BlockSpec((IN_TM, GROUP_WIDTH), lambda i: (i, 0)),
            pl.BlockSpec((IN_TM, GROUP_WIDTH), lambda i: (i, 0)),
        ],
        out_shape=[
            jax.ShapeDtypeStruct((m, n), BF16),
            jax.ShapeDtypeStruct((m, GROUP_WIDTH), F32),
            jax.ShapeDtypeStruct((m, GROUP_WIDTH), F32),
        ],
        compiler_params=_params(1),
        name="in_proj",
    )(x2d, g_pre, *([w_bf16] * IN_W_PARTS), cos_t, sin_t)


SB_T = 256
SB_HEADS_PER_STEP = 8
EXP_UNDERFLOW = -104.0
SWITCH_OFF = -1e30


def _sb_prompt_kernel(q_ref, k_ref, v_ref, gate_ref, g_ref, o_ref, vt_ref):
    i = pl.program_id(2)
    n_kb = vt_ref.shape[0]
    heads = range(SB_HEADS_PER_STEP)
    lanes = lambda hh: slice(hh * HEAD_DIM, (hh + 1) * HEAD_DIM)

    @pl.when(i == 0)
    def _():
        for kb in range(n_kb):
            vt_ref[kb] = v_ref[kb * SB_T:(kb + 1) * SB_T, :].astype(F32).T.astype(BF16)

    row = lax.broadcasted_iota(jnp.int32, (SB_T, SB_T), 0)
    col = lax.broadcasted_iota(jnp.int32, (SB_T, SB_T), 1)
    minus_from = -(col >= row).astype(BF16)
    strict = row < col

    def sweep(blocks, carries, accs):
        zs, tails = [], []
        for kb, _, _ in blocks:
            start = pl.multiple_of(kb * SB_T, SB_T)
            zs.append([lax.dot_general(k_ref[pl.ds(start, SB_T), lanes(hh)], q_ref[:, lanes(hh)], _NT,
                                       preferred_element_type=F32) for hh in heads])
        for (_, diagonal, _), z_heads in zip(blocks, zs):
            drops = []
            for z in z_heads:
                zb = z.astype(BF16)
                drop = jnp.maximum(zb, 0.0) + jnp.log(1.0 + jnp.exp(-jnp.abs(zb)))
                if diagonal:
                    drop = jnp.where(strict, drop, jnp.zeros_like(drop))
                drops.append(drop)
            tails.append([jnp.dot(minus_from, drop, preferred_element_type=F32) for drop in drops])
        for (kb, diagonal, offset), z_heads, tail_heads in zip(blocks, zs, tails):
            if offset is not None:
                carries = [c + offset for c in carries]
            probs = []
            for hh in heads:
                a = jnp.exp(z_heads[hh] + tail_heads[hh] + carries[hh])
                if diagonal:
                    a = jnp.where(strict, a, 0.0)
                probs.append(a.astype(BF16))
            accs = [accs[hh] + jnp.dot(vt_ref[kb, lanes(hh), :], probs[hh], preferred_element_type=F32)
                    for hh in heads]
            carries = [carries[hh] + tail_heads[hh][0:1, :] for hh in heads]
        return carries, accs

    def alive(carries):
        return functools.reduce(jnp.maximum, [jnp.max(c) for c in carries]) > EXP_UNDERFLOW

    previous_off = jnp.where(i > 0, 0.0, SWITCH_OFF).astype(F32)
    carries, accs = sweep([(i, True, None), (jnp.maximum(i - 1, 0), False, previous_off)],
                          [jnp.zeros((1, SB_T), F32) for _ in heads],
                          [jnp.zeros((HEAD_DIM, SB_T), F32) for _ in heads])

    def cond(c):
        return (c[0] < i) & c[1]

    def body(c):
        s, _, carries, accs = c
        carries, accs = sweep([(i - 1 - s, False, None)], carries, accs)
        return s + 1, alive(carries), carries, accs

    _, _, _, accs = lax.while_loop(cond, body, (jnp.int32(1), alive(carries), carries, accs))
    for hh in heads:
        acc = accs[hh]
        normed = (acc * lax.rsqrt(jnp.mean(acc * acc, axis=0, keepdims=True) + EPS)).T
        gate = gate_ref[:, lanes(hh)].astype(F32)
        o_ref[:, lanes(hh)] = (normed * g_ref[:, lanes(hh)] * (gate * jax.nn.sigmoid(gate))).astype(BF16)


def _sb_prompt(p3, g_sb):
    b, t, _ = p3.shape
    w = SB_HEADS_PER_STEP * HEAD_DIM
    sb = GROUP_WIDTH // w
    return pl.pallas_call(
        _sb_prompt_kernel,
        grid=(b, sb, t // SB_T),
        in_specs=[
            pl.BlockSpec((None, SB_T, w), lambda b_, h, i: (b_, i, h)),
            pl.BlockSpec((None, t, w), lambda b_, h, i: (b_, 0, sb + h)),
            pl.BlockSpec((None, t, w), lambda b_, h, i: (b_, 0, 2 * sb + h)),
            pl.BlockSpec((None, SB_T, w), lambda b_, h, i: (b_, i, 3 * sb + h)),
            pl.BlockSpec((1, w), lambda b_, h, i: (0, h)),
        ],
        out_specs=pl.BlockSpec((None, SB_T, w), lambda b_, h, i: (b_, i, h)),
        out_shape=jax.ShapeDtypeStruct((b, t, GROUP_WIDTH), BF16),
        scratch_shapes=[pltpu.VMEM((t // SB_T, w, SB_T), BF16)],
        compiler_params=_params(3),
        name="sb_prompt",
    )(p3, p3, p3, p3, g_sb)


RET_L = 256


RET_HEADS_PER_STEP = 4


def _ret_prompt_kernel(lg_ref, q_ref, k_ref, v_ref, gate_ref, g_ref, o_ref, s_ref,
                       w_ref, read_ref, write_ref):
    group = pl.program_id(0)
    t = q_ref.shape[0]
    heads = range(RET_HEADS_PER_STEP)
    lanes = lambda hh: slice(hh * HEAD_DIM, (hh + 1) * HEAD_DIM)
    log_gamma = [lg_ref[group * RET_HEADS_PER_STEP + hh] for hh in heads]

    @pl.when(pl.program_id(1) == 0)
    def _():
        ii = lax.broadcasted_iota(jnp.int32, (RET_L, RET_L), 0)
        jj = lax.broadcasted_iota(jnp.int32, (RET_L, RET_L), 1)
        dist = jnp.abs(ii - jj).astype(F32)
        visible = jj // CHUNK <= ii // CHUNK
        idx = lax.broadcasted_iota(jnp.int32, (RET_L, HEAD_DIM), 0).astype(F32)
        for hh in heads:
            w_ref[hh] = jnp.where(visible, jnp.exp(log_gamma[hh] * dist), 0.0)
            read_ref[hh] = jnp.exp(log_gamma[hh] * (idx + 1.0))
            write_ref[hh] = jnp.exp(log_gamma[hh] * (RET_L - 1.0 - idx))

    decay_block = [jnp.exp(jnp.full((1, HEAD_DIM), log_gamma[hh] * RET_L, F32)) for hh in heads]

    def body(blk, states):
        rows = pl.ds(pl.multiple_of(blk * RET_L, RET_L), RET_L)
        qs = [q_ref[rows, lanes(hh)] for hh in heads]
        ks = [k_ref[rows, lanes(hh)] for hh in heads]
        vs = [v_ref[rows, lanes(hh)] for hh in heads]
        scores = [lax.dot_general(qs[hh], ks[hh], _NT, preferred_element_type=F32) for hh in heads]
        reads = [jnp.dot(qs[hh], states[hh].astype(BF16), preferred_element_type=F32) for hh in heads]
        kws = [(ks[hh].astype(F32) * write_ref[hh]).astype(BF16) for hh in heads]
        writes = [lax.dot_general(kws[hh], vs[hh], _TN, preferred_element_type=F32) for hh in heads]
        weighted = [(scores[hh] * w_ref[hh]).astype(BF16) for hh in heads]
        outs = [jnp.dot(weighted[hh], vs[hh], preferred_element_type=F32) + reads[hh] * read_ref[hh]
                for hh in heads]
        for hh in heads:
            o_ref[rows, lanes(hh)] = _head_norm_gate(
                outs[hh], g_ref[:, lanes(hh)], gate_ref[rows, lanes(hh)]).astype(BF16)
        return [states[hh] * decay_block[hh] + writes[hh] for hh in heads]

    states = lax.fori_loop(0, t // RET_L, body, [jnp.zeros((HEAD_DIM, HEAD_DIM), F32) for _ in heads])
    for hh in heads:
        s_ref[hh] = states[hh]


def _ret_prompt(log_gamma, p3, g_r):
    b, t, _ = p3.shape
    w = RET_HEADS_PER_STEP * HEAD_DIM
    groups = GROUP_WIDTH // w
    col = lambda seg: (lambda g_, b_, lg: (b_, 0, seg * groups + g_))
    return pl.pallas_call(
        _ret_prompt_kernel,
        grid_spec=pltpu.PrefetchScalarGridSpec(
            num_scalar_prefetch=1,
            grid=(groups, b),
            in_specs=[
                pl.BlockSpec((None, t, w), col(4)),
                pl.BlockSpec((None, t, w), col(5)),
                pl.BlockSpec((None, t, w), col(6)),
                pl.BlockSpec((None, t, w), col(7)),
                pl.BlockSpec((1, w), lambda g_, b_, lg: (0, g_)),
            ],
            out_specs=[
                pl.BlockSpec((None, t, w), lambda g_, b_, lg: (b_, 0, g_)),
                pl.BlockSpec((None, RET_HEADS_PER_STEP, HEAD_DIM, HEAD_DIM),
                             lambda g_, b_, lg: (b_, g_, 0, 0)),
            ],
            scratch_shapes=[
                pltpu.VMEM((RET_HEADS_PER_STEP, RET_L, RET_L), F32),
                pltpu.VMEM((RET_HEADS_PER_STEP, RET_L, HEAD_DIM), F32),
                pltpu.VMEM((RET_HEADS_PER_STEP, RET_L, HEAD_DIM), F32),
            ],
        ),
        out_shape=[
            jax.ShapeDtypeStruct((b, t, GROUP_WIDTH), BF16),
            jax.ShapeDtypeStruct((b, N_HEADS, HEAD_DIM, HEAD_DIM), F32),
        ],
        compiler_params=_params(2),
        name="ret_prompt",
    )(log_gamma, p3, p3, p3, p3, g_r)


SBS_TK = 256
SBS_PREFETCH_SLOTS = 2


def _sb_sample_kernel(q_ref, kn_ref, vn_ref, gate_ref, g_ref, kc_hbm, vc_hbm, o_ref,
                      qbd_ref, kbuf, vbuf, sems):
    b = pl.program_id(0)
    n = q_ref.shape[0]
    lanes = N_HEADS * n
    block_rows = SBS_TK * N_HEADS
    n_blocks = kc_hbm.shape[1] // block_rows
    spare = SBS_PREFETCH_SLOTS

    def copies(stream, blk, slot):
        rows = pl.ds((n_blocks - 1 - blk) * block_rows, block_rows)
        return (pltpu.make_async_copy(kc_hbm.at[stream, rows], kbuf.at[slot], sems.at[0, slot]),
                pltpu.make_async_copy(vc_hbm.at[stream, rows], vbuf.at[slot], sems.at[1, slot]))

    def start(pair):
        pair[0].start()
        pair[1].start()

    def wait(pair):
        pair[0].wait()
        pair[1].wait()

    slot = b % SBS_PREFETCH_SLOTS

    @pl.when(b == 0)
    def _():
        start(copies(0, 0, 0))

    @pl.when(b + 1 < pl.num_programs(0))
    def _():
        start(copies(b + 1, 0, (b + 1) % SBS_PREFETCH_SLOTS))

    q = q_ref[...]
    qt = jnp.concatenate([q] * N_HEADS, axis=0)
    rr = lax.broadcasted_iota(jnp.int32, (lanes, GROUP_WIDTH), 0) // n
    cc = lax.broadcasted_iota(jnp.int32, (lanes, GROUP_WIDTH), 1) // HEAD_DIM
    qbd_ref[...] = jnp.where(rr == cc, qt, jnp.zeros_like(qt))

    z = lax.dot_general(kn_ref[...], qbd_ref[...], _NT, preferred_element_type=F32)
    ki = lax.broadcasted_iota(jnp.int32, (n, lanes), 0)
    qi = lax.broadcasted_iota(jnp.int32, (n, lanes), 1) % n
    strict = ki < qi
    log_beta, log_keep = _log_sigmoids(z)
    lk = jnp.where(strict, log_keep, 0.0)
    mj = lax.broadcasted_iota(jnp.int32, (n, n), 0)
    mm = lax.broadcasted_iota(jnp.int32, (n, n), 1)
    later = jnp.dot((mm > mj).astype(F32), lk, preferred_element_type=F32)
    a = jnp.where(strict, jnp.exp(log_beta + later), 0.0)
    acc = lax.dot_general(a.astype(BF16), vn_ref[...], _TN, preferred_element_type=F32)
    carry = jnp.sum(lk, axis=0, keepdims=True)

    kj = lax.broadcasted_iota(jnp.int32, (SBS_TK, SBS_TK), 0)
    km = lax.broadcasted_iota(jnp.int32, (SBS_TK, SBS_TK), 1)
    later_keys = (km > kj).astype(BF16)

    def cached_block(slot, carry, acc):
        def all_heads(buf):
            return jnp.concatenate(
                [buf[slot, pl.ds(hh, SBS_TK, stride=N_HEADS), :].astype(BF16) for hh in range(N_HEADS)],
                axis=1)

        z = lax.dot_general(all_heads(kbuf), qbd_ref[...], _NT, preferred_element_type=F32)
        log_beta, log_keep = _log_sigmoids(z)
        lk = log_keep.astype(BF16)
        later = jnp.dot(later_keys, lk, preferred_element_type=F32)
        a = jnp.exp(log_beta + later + carry)
        acc = acc + lax.dot_general(a.astype(BF16), all_heads(vbuf), _TN, preferred_element_type=F32)
        return carry + later[0:1, :] + lk[0:1, :].astype(F32), acc

    wait(copies(b, 0, slot))
    carry, acc = cached_block(slot, carry, acc)

    def cond(c):
        return (c[0] < n_blocks) & c[1]

    def body(c):
        blk, _, carry, acc = c
        pair = copies(b, blk, spare)
        start(pair)
        wait(pair)
        carry, acc = cached_block(spare, carry, acc)
        return blk + 1, jnp.max(carry) > EXP_UNDERFLOW, carry, acc

    _, _, _, acc = lax.while_loop(
        cond, body, (jnp.int32(1), jnp.max(carry) > EXP_UNDERFLOW, carry, acc))

    for hh in range(N_HEADS):
        cols = slice(hh * HEAD_DIM, (hh + 1) * HEAD_DIM)
        o = acc[hh * n:(hh + 1) * n, cols]
        o_ref[:, cols] = _head_norm_gate(o, g_ref[:, cols], gate_ref[:, cols]).astype(BF16)


def _sb_sample(p3, g_sb, cache_k, cache_v):
    b, n, _ = p3.shape
    seg = lambda sidx: pl.BlockSpec((None, n, GROUP_WIDTH), lambda b_: (b_, 0, sidx))
    slots = SBS_PREFETCH_SLOTS + 1
    return pl.pallas_call(
        _sb_sample_kernel,
        grid=(b,),
        in_specs=[seg(0), seg(1), seg(2), seg(3),
                  pl.BlockSpec((1, GROUP_WIDTH), lambda b_: (0, 0)),
                  pl.BlockSpec(memory_space=pl.ANY),
                  pl.BlockSpec(memory_space=pl.ANY)],
        out_specs=pl.BlockSpec((None, n, GROUP_WIDTH), lambda b_: (b_, 0, 0)),
        out_shape=jax.ShapeDtypeStruct((b, n, GROUP_WIDTH), BF16),
        scratch_shapes=[
            pltpu.VMEM((N_HEADS * n, GROUP_WIDTH), BF16),
            pltpu.VMEM((slots, SBS_TK * N_HEADS, HEAD_DIM), F32),
            pltpu.VMEM((slots, SBS_TK * N_HEADS, HEAD_DIM), F32),
            pltpu.SemaphoreType.DMA((2, slots)),
        ],
        compiler_params=_params(1),
        name="sb_sample",
    )(p3, p3, p3, p3, g_sb, cache_k, cache_v)


def _ret_sample_kernel(lg_ref, q_ref, k_ref, v_ref, gate_ref, g_ref, s_ref, o_ref, so_ref):
    n = q_ref.shape[0]
    ii = lax.broadcasted_iota(jnp.int32, (n, n), 0)
    jj = lax.broadcasted_iota(jnp.int32, (n, n), 1)
    dist = jnp.abs(ii - jj).astype(F32)
    idx = lax.broadcasted_iota(jnp.int32, (n, 1), 0).astype(F32)
    heads = range(N_HEADS)
    cols = [slice(hh * HEAD_DIM, (hh + 1) * HEAD_DIM) for hh in heads]
    lg = [lg_ref[hh] for hh in heads]
    qs = [q_ref[:, c] for c in cols]
    ks = [k_ref[:, c] for c in cols]
    vs = [v_ref[:, c] for c in cols]
    scores = [lax.dot_general(qs[hh], ks[hh], _NT, preferred_element_type=F32) for hh in heads]
    reads = [jnp.dot(qs[hh], s_ref[hh].astype(BF16), preferred_element_type=F32) for hh in heads]
    kws = [(ks[hh].astype(F32) * jnp.exp(lg[hh] * (n - 1.0 - idx))).astype(BF16) for hh in heads]
    writes = [lax.dot_general(kws[hh], vs[hh], _TN, preferred_element_type=F32) for hh in heads]
    weighted = [(scores[hh] * jnp.exp(lg[hh] * dist)).astype(BF16) for hh in heads]
    outs = [jnp.dot(weighted[hh], vs[hh], preferred_element_type=F32)
            + reads[hh] * jnp.exp(lg[hh] * (idx + 1.0)) for hh in heads]
    for hh in heads:
        decay_all = jnp.exp(jnp.full((1, HEAD_DIM), lg[hh] * n, F32))
        so_ref[hh] = s_ref[hh] * decay_all + writes[hh]
        o_ref[:, cols[hh]] = _head_norm_gate(
            outs[hh], g_ref[:, cols[hh]], gate_ref[:, cols[hh]]).astype(BF16)


def _ret_sample(log_gamma, p3, g_r, state):
    b, n, _ = p3.shape
    seg = lambda sidx: pl.BlockSpec((None, n, GROUP_WIDTH), lambda b_, lg: (b_, 0, sidx))
    state_spec = pl.BlockSpec((None, N_HEADS, HEAD_DIM, HEAD_DIM), lambda b_, lg: (b_, 0, 0, 0))
    return pl.pallas_call(
        _ret_sample_kernel,
        grid_spec=pltpu.PrefetchScalarGridSpec(
            num_scalar_prefetch=1,
            grid=(b,),
            in_specs=[seg(4), seg(5), seg(6), seg(7),
                      pl.BlockSpec((1, GROUP_WIDTH), lambda b_, lg: (0, 0)),
                      state_spec],
            out_specs=[pl.BlockSpec((None, n, GROUP_WIDTH), lambda b_, lg: (b_, 0, 0)), state_spec],
        ),
        out_shape=[
            jax.ShapeDtypeStruct((b, n, GROUP_WIDTH), BF16),
            jax.ShapeDtypeStruct(state.shape, F32),
        ],
        compiler_params=_params(1),
        name="ret_sample",
    )(log_gamma, p3, p3, p3, p3, g_r, state)


OUT_TM = 512


def _out_proj_kernel(ysb_ref, yr_ref, w_ref, g_ref, x_ref, o_ref):
    out = jnp.dot(ysb_ref[...], w_ref[0:GROUP_WIDTH, :], preferred_element_type=F32)
    out = out + jnp.dot(yr_ref[...], w_ref[GROUP_WIDTH:2 * GROUP_WIDTH, :], preferred_element_type=F32)
    r = lax.rsqrt(jnp.mean(out * out, axis=-1, keepdims=True) + EPS)
    o_ref[...] = x_ref[...] + out * r * g_ref[...]


def _out_proj(y_sb, y_r, w_bf16, g_post, x2d):
    m, d = x2d.shape
    tm = min(OUT_TM, m)
    row = lambda width: pl.BlockSpec((tm, width), lambda i: (i, 0))
    return pl.pallas_call(
        _out_proj_kernel,
        grid=(m // tm,),
        in_specs=[row(GROUP_WIDTH), row(GROUP_WIDTH),
                  pl.BlockSpec(w_bf16.shape, lambda i: (0, 0)),
                  pl.BlockSpec((1, d), lambda i: (0, 0)),
                  row(d)],
        out_specs=row(d),
        out_shape=jax.ShapeDtypeStruct((m, d), F32),
        compiler_params=_params(1),
        name="out_proj",
    )(y_sb, y_r, w_bf16, g_post, x2d)


def _epilogue_tables(positions):
    half = HEAD_DIM // 2
    inv = ROPE_BASE ** (-np.arange(half, dtype=np.float64) / half)
    ang = positions.astype(np.float64)[:, None] * inv[None, :]
    cos = np.concatenate([np.cos(ang), np.cos(ang)], axis=-1)
    sin = np.concatenate([-np.sin(ang), np.sin(ang)], axis=-1)
    one, zero = np.ones_like(cos), np.zeros_like(sin)
    cos4 = np.stack([one * Q_SCALE, one, cos, cos * Q_SCALE])
    sin4 = np.stack([zero, zero, sin, sin * Q_SCALE])
    return jnp.asarray(cos4, F32), jnp.asarray(sin4, F32)


def kernel(x_prompt, x_sample, cache_sb_k, cache_sb_v, state_ret, norm_pre, w_in, sb_head_norm,
           ret_head_norm, w_out, norm_post):
    depth = norm_pre.shape[0]
    assert depth == 1
    b_p, t_p, d = x_prompt.shape
    b_s, t_s, _ = x_sample.shape
    past = cache_sb_k.shape[2]
    log_gamma = jnp.asarray(np.log(1.0 - 2.0 ** (-5.0 - np.arange(N_HEADS))), F32)

    l = 0
    w_in_b = w_in[l].astype(BF16)
    w_out_b = w_out[l].astype(BF16)
    g_pre = norm_pre[l][None, :]
    g_post = norm_post[l][None, :]
    g_sb = sb_head_norm[l][None, :]
    g_r = ret_head_norm[l][None, :]

    cos_p, sin_p = _epilogue_tables(np.arange(t_p))
    xp2 = x_prompt.reshape(b_p * t_p, d)
    p_p, k_p, v_p = _in_proj(xp2, g_pre, w_in_b, cos_p, sin_p)
    p3 = p_p.reshape(b_p, t_p, N_SEGMENTS * GROUP_WIDTH)
    y_sb = _sb_prompt(p3, g_sb)
    y_r, s_p = _ret_prompt(log_gamma, p3, g_r)
    y_p = _out_proj(y_sb.reshape(b_p * t_p, GROUP_WIDTH), y_r.reshape(b_p * t_p, GROUP_WIDTH),
                    w_out_b, g_post, xp2).reshape(b_p, t_p, d)

    cos_s, sin_s = _epilogue_tables(np.tile(past + np.arange(t_s), b_s))
    xs2 = x_sample.reshape(b_s * t_s, d)
    p_s, k_s, v_s = _in_proj(xs2, g_pre, w_in_b, cos_s, sin_s)
    ps3 = p_s.reshape(b_s, t_s, N_SEGMENTS * GROUP_WIDTH)
    ys_sb = _sb_sample(ps3, g_sb, cache_sb_k[l].reshape(b_s, past * N_HEADS, HEAD_DIM),
                       cache_sb_v[l].reshape(b_s, past * N_HEADS, HEAD_DIM))
    ys_r, s_s = _ret_sample(log_gamma, ps3, g_r, state_ret[l])
    y_s = _out_proj(ys_sb.reshape(b_s * t_s, GROUP_WIDTH), ys_r.reshape(b_s * t_s, GROUP_WIDTH),
                    w_out_b, g_post, xs2).reshape(b_s, t_s, d)

    heads = lambda a, bb, tt: a.reshape(1, bb, tt, N_HEADS, HEAD_DIM)
    return (y_p, y_s, heads(k_p, b_p, t_p), heads(v_p, b_p, t_p), s_p[None],
            heads(k_s, b_s, t_s), heads(v_s, b_s, t_s), s_s[None])
```

```python
import functools

import numpy as np
import jax
import jax.numpy as jnp
from jax import lax
from jax.experimental import pallas as pl
from jax.experimental.pallas import tpu as pltpu

HEAD_DIM = 128
N_HEADS = 8
GROUP_WIDTH = N_HEADS * HEAD_DIM
N_SEGMENTS = 8
CHUNK = 64
ROPE_BASE = 10000.0
EPS = 1e-6
Q_SCALE = HEAD_DIM ** -0.5

F32 = jnp.float32
BF16 = jnp.bfloat16

_NT = (((1,), (1,)), ((), ()))
_TN = (((0,), (0,)), ((), ()))

VMEM_LIMIT = 56 * 1024 * 1024


def _params(n_axes):
    return pltpu.CompilerParams(
        dimension_semantics=("arbitrary",) * n_axes, vmem_limit_bytes=VMEM_LIMIT)


def _log_sigmoids(z):
    l = jnp.log(1.0 + jnp.exp(-jnp.abs(z)))
    log_beta = jnp.minimum(z, 0.0) - l
    return log_beta, log_beta - z


def _head_norm_gate(o, g, gate):
    y = o * lax.rsqrt(jnp.mean(o * o, axis=-1, keepdims=True) + EPS) * g
    gf = gate.astype(F32)
    return y * (gf * jax.nn.sigmoid(gf))


IN_TM = 256
IN_TN = 512
_SCALE_Q, _IDENTITY, _ROTARY, _ROTARY_SCALED = range(4)
_TABLE_OF_SEGMENT = (_SCALE_Q, _IDENTITY, _IDENTITY, _IDENTITY, _ROTARY, _ROTARY_SCALED, _IDENTITY, _IDENTITY)
_K_SEGMENT, _V_SEGMENT = 1, 2


def _in_proj_kernel(x_ref, g_ref, w_ref, cos_ref, sin_ref, p_ref, k_ref, v_ref):
    x = x_ref[...]
    r = lax.rsqrt(jnp.mean(x * x, axis=-1, keepdims=True) + EPS)
    h = (x * r * g_ref[...]).astype(BF16)
    for seg, table in enumerate(_TABLE_OF_SEGMENT):
        for c0 in range(0, GROUP_WIDTH, IN_TN):
            col0 = seg * GROUP_WIDTH + c0
            acc = jnp.dot(h, w_ref[:, col0:col0 + IN_TN], preferred_element_type=F32)
            for hh in range(IN_TN // HEAD_DIM):
                xs = acc[:, hh * HEAD_DIM:(hh + 1) * HEAD_DIM]
                y = xs * cos_ref[table]
                if table in (_ROTARY, _ROTARY_SCALED):
                    y = y + pltpu.roll(xs, HEAD_DIM // 2, 1) * sin_ref[table]
                p_ref[:, col0 + hh * HEAD_DIM:col0 + (hh + 1) * HEAD_DIM] = y.astype(BF16)
                head_rows = pl.ds(c0 // HEAD_DIM + hh, IN_TM, stride=N_HEADS)
                if seg == _K_SEGMENT:
                    k_ref[head_rows, :] = xs
                if seg == _V_SEGMENT:
                    v_ref[head_rows, :] = xs


def _in_proj(x2d, g_pre, w_bf16, cos_t, sin_t):
    m, d = x2d.shape
    n = w_bf16.shape[1]
    t_blocks = cos_t.shape[1] // IN_TM
    table_spec = pl.BlockSpec((len(cos_t), IN_TM, HEAD_DIM), lambda i: (0, i % t_blocks, 0))
    return pl.pallas_call(
        _in_proj_kernel,
        grid=(m // IN_TM,),
        in_specs=[
            pl.BlockSpec((IN_TM, d), lambda i: (i, 0)),
            pl.BlockSpec((1, d), lambda i: (0, 0)),
            pl.BlockSpec((d, n), lambda i: (0, 0), pipeline_mode=pl.Buffered(1)),
            table_spec,
            table_spec,
        ],
        out_specs=[
            pl.BlockSpec((IN_TM, n), lambda i: (i, 0)),
            pl.BlockSpec((IN_TM * N_HEADS, HEAD_DIM), lambda i: (i, 0)),
            pl.BlockSpec((IN_TM * N_HEADS, HEAD_DIM), lambda i: (i, 0)),
        ],
        out_shape=[
            jax.ShapeDtypeStruct((m, n), BF16),
            jax.ShapeDtypeStruct((m * N_HEADS, HEAD_DIM), F32),
            jax.ShapeDtypeStruct((m * N_HEADS, HEAD_DIM), F32),
        ],
        compiler_params=_params(1),
        name="in_proj",
    )(x2d, g_pre, w_bf16, cos_t, sin_t)


SB_T = 256
SB_HEADS_PER_STEP = 8
EXP_UNDERFLOW = -104.0
SWITCH_OFF = -1e30


def _sb_prompt_kernel(q_ref, k_ref, v_ref, gate_ref, g_ref, o_ref, vt_ref):
    i = pl.program_id(2)
    n_kb = vt_ref.shape[0]
    heads = range(SB_HEADS_PER_STEP)
    lanes = lambda hh: slice(hh * HEAD_DIM, (hh + 1) * HEAD_DIM)

    @pl.when(i == 0)
    def _():
        for kb in range(n_kb):
            vt_ref[kb] = v_ref[kb * SB_T:(kb + 1) * SB_T, :].astype(F32).T.astype(BF16)

    row = lax.broadcasted_iota(jnp.int32, (SB_T, SB_T), 0)
    col = lax.broadcasted_iota(jnp.int32, (SB_T, SB_T), 1)
    later_keys = (col > row).astype(BF16)
    strict = row < col

    def sweep(blocks, carries, accs):
        zs, log_betas, keeps, laters = [], [], [], []
        for kb, _, _ in blocks:
            start = pl.multiple_of(kb * SB_T, SB_T)
            zs.append([lax.dot_general(k_ref[pl.ds(start, SB_T), lanes(hh)], q_ref[:, lanes(hh)], _NT,
                                       preferred_element_type=F32) for hh in heads])
        for (_, diagonal, _), z_heads in zip(blocks, zs):
            lb_heads, keep_heads = [], []
            for z in z_heads:
                log_beta, log_keep = _log_sigmoids(z)
                if diagonal:
                    log_keep = jnp.where(strict, log_keep, 0.0)
                lb_heads.append(log_beta)
                keep_heads.append(log_keep.astype(BF16))
            log_betas.append(lb_heads)
            keeps.append(keep_heads)
            laters.append([jnp.dot(later_keys, keep, preferred_element_type=F32) for keep in keep_heads])
        for (kb, diagonal, offset), lb_heads, keep_heads, later_heads in zip(blocks, log_betas, keeps, laters):
            if offset is not None:
                carries = [c + offset for c in carries]
            probs = []
            for hh in heads:
                a = jnp.exp(lb_heads[hh] + later_heads[hh] + carries[hh])
                if diagonal:
                    a = jnp.where(strict, a, 0.0)
                probs.append(a.astype(BF16))
            accs = [accs[hh] + jnp.dot(vt_ref[kb, lanes(hh), :], probs[hh], preferred_element_type=F32)
                    for hh in heads]
            carries = [carries[hh] + later_heads[hh][0:1, :] + keep_heads[hh][0:1, :].astype(F32)
                       for hh in heads]
        return carries, accs

    def alive(carries):
        return functools.reduce(jnp.maximum, [jnp.max(c) for c in carries]) > EXP_UNDERFLOW

    previous_off = jnp.where(i > 0, 0.0, SWITCH_OFF).astype(F32)
    carries, accs = sweep([(i, True, None), (jnp.maximum(i - 1, 0), False, previous_off)],
                          [jnp.zeros((1, SB_T), F32) for _ in heads],
                          [jnp.zeros((HEAD_DIM, SB_T), F32) for _ in heads])

    def cond(c):
        return (c[0] < i) & c[1]

    def body(c):
        s, _, carries, accs = c
        carries, accs = sweep([(i - 1 - s, False, None)], carries, accs)
        return s + 1, alive(carries), carries, accs

    _, _, _, accs = lax.while_loop(cond, body, (jnp.int32(1), alive(carries), carries, accs))
    for hh in heads:
        acc = accs[hh]
        normed = (acc * lax.rsqrt(jnp.mean(acc * acc, axis=0, keepdims=True) + EPS)).T
        gate = gate_ref[:, lanes(hh)].astype(F32)
        o_ref[:, lanes(hh)] = (normed * g_ref[:, lanes(hh)] * (gate * jax.nn.sigmoid(gate))).astype(BF16)


def _sb_prompt(p3, g_sb):
    b, t, _ = p3.shape
    w = SB_HEADS_PER_STEP * HEAD_DIM
    sb = GROUP_WIDTH // w
    return pl.pallas_call(
        _sb_prompt_kernel,
        grid=(b, sb, t // SB_T),
        in_specs=[
            pl.BlockSpec((None, SB_T, w), lambda b_, h, i: (b_, i, h)),
            pl.BlockSpec((None, t, w), lambda b_, h, i: (b_, 0, sb + h)),
            pl.BlockSpec((None, t, w), lambda b_, h, i: (b_, 0, 2 * sb + h)),
            pl.BlockSpec((None, SB_T, w), lambda b_, h, i: (b_, i, 3 * sb + h)),
            pl.BlockSpec((1, w), lambda b_, h, i: (0, h)),
        ],
        out_specs=pl.BlockSpec((None, SB_T, w), lambda b_, h, i: (b_, i, h)),
        out_shape=jax.ShapeDtypeStruct((b, t, GROUP_WIDTH), BF16),
        scratch_shapes=[pltpu.VMEM((t // SB_T, w, SB_T), BF16)],
        compiler_params=_params(3),
        name="sb_prompt",
    )(p3, p3, p3, p3, g_sb)


RET_L = 256


RET_HEADS_PER_STEP = 4


def _ret_prompt_kernel(lg_ref, q_ref, k_ref, v_ref, gate_ref, g_ref, o_ref, s_ref,
                       w_ref, read_ref, write_ref):
    group = pl.program_id(0)
    t = q_ref.shape[0]
    heads = range(RET_HEADS_PER_STEP)
    lanes = lambda hh: slice(hh * HEAD_DIM, (hh + 1) * HEAD_DIM)
    log_gamma = [lg_ref[group * RET_HEADS_PER_STEP + hh] for hh in heads]

    @pl.when(pl.program_id(1) == 0)
    def _():
        ii = lax.broadcasted_iota(jnp.int32, (RET_L, RET_L), 0)
        jj = lax.broadcasted_iota(jnp.int32, (RET_L, RET_L), 1)
        dist = jnp.abs(ii - jj).astype(F32)
        visible = jj // CHUNK <= ii // CHUNK
        idx = lax.broadcasted_iota(jnp.int32, (RET_L, HEAD_DIM), 0).astype(F32)
        for hh in heads:
            w_ref[hh] = jnp.where(visible, jnp.exp(log_gamma[hh] * dist), 0.0)
            read_ref[hh] = jnp.exp(log_gamma[hh] * (idx + 1.0))
            write_ref[hh] = jnp.exp(log_gamma[hh] * (RET_L - 1.0 - idx))

    decay_block = [jnp.exp(jnp.full((1, HEAD_DIM), log_gamma[hh] * RET_L, F32)) for hh in heads]

    def body(blk, states):
        rows = pl.ds(pl.multiple_of(blk * RET_L, RET_L), RET_L)
        qs = [q_ref[rows, lanes(hh)] for hh in heads]
        ks = [k_ref[rows, lanes(hh)] for hh in heads]
        vs = [v_ref[rows, lanes(hh)] for hh in heads]
        scores = [lax.dot_general(qs[hh], ks[hh], _NT, preferred_element_type=F32) for hh in heads]
        reads = [jnp.dot(qs[hh], states[hh].astype(BF16), preferred_element_type=F32) for hh in heads]
        kws = [(ks[hh].astype(F32) * write_ref[hh]).astype(BF16) for hh in heads]
        writes = [lax.dot_general(kws[hh], vs[hh], _TN, preferred_element_type=F32) for hh in heads]
        weighted = [(scores[hh] * w_ref[hh]).astype(BF16) for hh in heads]
        outs = [jnp.dot(weighted[hh], vs[hh], preferred_element_type=F32) + reads[hh] * read_ref[hh]
                for hh in heads]
        for hh in heads:
            o_ref[rows, lanes(hh)] = _head_norm_gate(
                outs[hh], g_ref[:, lanes(hh)], gate_ref[rows, lanes(hh)]).astype(BF16)
        return [states[hh] * decay_block[hh] + writes[hh] for hh in heads]

    states = lax.fori_loop(0, t // RET_L, body, [jnp.zeros((HEAD_DIM, HEAD_DIM), F32) for _ in heads])
    for hh in heads:
        s_ref[hh] = states[hh]


def _ret_prompt(log_gamma, p3, g_r):
    b, t, _ = p3.shape
    w = RET_HEADS_PER_STEP * HEAD_DIM
    groups = GROUP_WIDTH // w
    col = lambda seg: (lambda g_, b_, lg: (b_, 0, seg * groups + g_))
    return pl.pallas_call(
        _ret_prompt_kernel,
        grid_spec=pltpu.PrefetchScalarGridSpec(
            num_scalar_prefetch=1,
            grid=(groups, b),
            in_specs=[
                pl.BlockSpec((None, t, w), col(4)),
                pl.BlockSpec((None, t, w), col(5)),
                pl.BlockSpec((None, t, w), col(6)),
                pl.BlockSpec((None, t, w), col(7)),
                pl.BlockSpec((1, w), lambda g_, b_, lg: (0, g_)),
            ],
            out_specs=[
                pl.BlockSpec((None, t, w), lambda g_, b_, lg: (b_, 0, g_)),
                pl.BlockSpec((None, RET_HEADS_PER_STEP, HEAD_DIM, HEAD_DIM),
                             lambda g_, b_, lg: (b_, g_, 0, 0)),
            ],
            scratch_shapes=[
                pltpu.VMEM((RET_HEADS_PER_STEP, RET_L, RET_L), F32),
                pltpu.VMEM((RET_HEADS_PER_STEP, RET_L, HEAD_DIM), F32),
                pltpu.VMEM((RET_HEADS_PER_STEP, RET_L, HEAD_DIM), F32),
            ],
        ),
        out_shape=[
            jax.ShapeDtypeStruct((b, t, GROUP_WIDTH), BF16),
            jax.ShapeDtypeStruct((b, N_HEADS, HEAD_DIM, HEAD_DIM), F32),
        ],
        compiler_params=_params(2),
        name="ret_prompt",
    )(log_gamma, p3, p3, p3, p3, g_r)


SBS_TK = 256
SBS_PREFETCH_SLOTS = 2


def _sb_sample_kernel(q_ref, kn_ref, vn_ref, gate_ref, g_ref, kc_hbm, vc_hbm, o_ref,
                      qbd_ref, kbuf, vbuf, sems):
    b = pl.program_id(0)
    n = q_ref.shape[0]
    lanes = N_HEADS * n
    block_rows = SBS_TK * N_HEADS
    n_blocks = kc_hbm.shape[1] // block_rows
    spare = SBS_PREFETCH_SLOTS

    def copies(stream, blk, slot):
        rows = pl.ds((n_blocks - 1 - blk) * block_rows, block_rows)
        return (pltpu.make_async_copy(kc_hbm.at[stream, rows], kbuf.at[slot], sems.at[0, slot]),
                pltpu.make_async_copy(vc_hbm.at[stream, rows], vbuf.at[slot], sems.at[1, slot]))

    def start(pair):
        pair[0].start()
        pair[1].start()

    def wait(pair):
        pair[0].wait()
        pair[1].wait()

    slot = b % SBS_PREFETCH_SLOTS

    @pl.when(b == 0)
    def _():
        start(copies(0, 0, 0))

    @pl.when(b + 1 < pl.num_programs(0))
    def _():
        start(copies(b + 1, 0, (b + 1) % SBS_PREFETCH_SLOTS))

    q = q_ref[...]
    qt = jnp.concatenate([q] * N_HEADS, axis=0)
    rr = lax.broadcasted_iota(jnp.int32, (lanes, GROUP_WIDTH), 0) // n
    cc = lax.broadcasted_iota(jnp.int32, (lanes, GROUP_WIDTH), 1) // HEAD_DIM
    qbd_ref[...] = jnp.where(rr == cc, qt, jnp.zeros_like(qt))

    z = lax.dot_general(kn_ref[...], qbd_ref[...], _NT, preferred_element_type=F32)
    ki = lax.broadcasted_iota(jnp.int32, (n, lanes), 0)
    qi = lax.broadcasted_iota(jnp.int32, (n, lanes), 1) % n
    strict = ki < qi
    log_beta, log_keep = _log_sigmoids(z)
    lk = jnp.where(strict, log_keep, 0.0)
    mj = lax.broadcasted_iota(jnp.int32, (n, n), 0)
    mm = lax.broadcasted_iota(jnp.int32, (n, n), 1)
    later = jnp.dot((mm > mj).astype(F32), lk, preferred_element_type=F32)
    a = jnp.where(strict, jnp.exp(log_beta + later), 0.0)
    acc = lax.dot_general(a.astype(BF16), vn_ref[...], _TN, preferred_element_type=F32)
    carry = jnp.sum(lk, axis=0, keepdims=True)

    kj = lax.broadcasted_iota(jnp.int32, (SBS_TK, SBS_TK), 0)
    km = lax.broadcasted_iota(jnp.int32, (SBS_TK, SBS_TK), 1)
    later_keys = (km > kj).astype(BF16)

    def cached_block(slot, carry, acc):
        def all_heads(buf):
            return jnp.concatenate(
                [buf[slot, pl.ds(hh, SBS_TK, stride=N_HEADS), :].astype(BF16) for hh in range(N_HEADS)],
                axis=1)

        z = lax.dot_general(all_heads(kbuf), qbd_ref[...], _NT, preferred_element_type=F32)
        log_beta, log_keep = _log_sigmoids(z)
        lk = log_keep.astype(BF16)
        later = jnp.dot(later_keys, lk, preferred_element_type=F32)
        a = jnp.exp(log_beta + later + carry)
        acc = acc + lax.dot_general(a.astype(BF16), all_heads(vbuf), _TN, preferred_element_type=F32)
        return carry + later[0:1, :] + lk[0:1, :].astype(F32), acc

    wait(copies(b, 0, slot))
    carry, acc = cached_block(slot, carry, acc)

    def cond(c):
        return (c[0] < n_blocks) & c[1]

    def body(c):
        blk, _, carry, acc = c
        pair = copies(b, blk, spare)
        start(pair)
        wait(pair)
        carry, acc = cached_block(spare, carry, acc)
        return blk + 1, jnp.max(carry) > EXP_UNDERFLOW, carry, acc

    _, _, _, acc = lax.while_loop(
        cond, body, (jnp.int32(1), jnp.max(carry) > EXP_UNDERFLOW, carry, acc))

    for hh in range(N_HEADS):
        cols = slice(hh * HEAD_DIM, (hh + 1) * HEAD_DIM)
        o = acc[hh * n:(hh + 1) * n, cols]
        o_ref[:, cols] = _head_norm_gate(o, g_ref[:, cols], gate_ref[:, cols]).astype(BF16)


def _sb_sample(p3, g_sb, cache_k, cache_v):
    b, n, _ = p3.shape
    seg = lambda sidx: pl.BlockSpec((None, n, GROUP_WIDTH), lambda b_: (b_, 0, sidx))
    slots = SBS_PREFETCH_SLOTS + 1
    return pl.pallas_call(
        _sb_sample_kernel,
        grid=(b,),
        in_specs=[seg(0), seg(1), seg(2), seg(3),
                  pl.BlockSpec((1, GROUP_WIDTH), lambda b_: (0, 0)),
                  pl.BlockSpec(memory_space=pl.ANY),
                  pl.BlockSpec(memory_space=pl.ANY)],
        out_specs=pl.BlockSpec((None, n, GROUP_WIDTH), lambda b_: (b_, 0, 0)),
        out_shape=jax.ShapeDtypeStruct((b, n, GROUP_WIDTH), BF16),
        scratch_shapes=[
            pltpu.VMEM((N_HEADS * n, GROUP_WIDTH), BF16),
            pltpu.VMEM((slots, SBS_TK * N_HEADS, HEAD_DIM), F32),
            pltpu.VMEM((slots, SBS_TK * N_HEADS, HEAD_DIM), F32),
            pltpu.SemaphoreType.DMA((2, slots)),
        ],
        compiler_params=_params(1),
        name="sb_sample",
    )(p3, p3, p3, p3, g_sb, cache_k, cache_v)


def _ret_sample_kernel(lg_ref, q_ref, k_ref, v_ref, gate_ref, g_ref, s_ref, o_ref, so_ref):
    n = q_ref.shape[0]
    ii = lax.broadcasted_iota(jnp.int32, (n, n), 0)
    jj = lax.broadcasted_iota(jnp.int32, (n, n), 1)
    dist = jnp.abs(ii - jj).astype(F32)
    idx = lax.broadcasted_iota(jnp.int32, (n, 1), 0).astype(F32)
    heads = range(N_HEADS)
    cols = [slice(hh * HEAD_DIM, (hh + 1) * HEAD_DIM) for hh in heads]
    lg = [lg_ref[hh] for hh in heads]
    qs = [q_ref[:, c] for c in cols]
    ks = [k_ref[:, c] for c in cols]
    vs = [v_ref[:, c] for c in cols]
    scores = [lax.dot_general(qs[hh], ks[hh], _NT, preferred_element_type=F32) for hh in heads]
    reads = [jnp.dot(qs[hh], s_ref[hh].astype(BF16), preferred_element_type=F32) for hh in heads]
    kws = [(ks[hh].astype(F32) * jnp.exp(lg[hh] * (n - 1.0 - idx))).astype(BF16) for hh in heads]
    writes = [lax.dot_general(kws[hh], vs[hh], _TN, preferred_element_type=F32) for hh in heads]
    weighted = [(scores[hh] * jnp.exp(lg[hh] * dist)).astype(BF16) for hh in heads]
    outs = [jnp.dot(weighted[hh], vs[hh], preferred_element_type=F32)
            + reads[hh] * jnp.exp(lg[hh] * (idx + 1.0)) for hh in heads]
    for hh in heads:
        decay_all = jnp.exp(jnp.full((1, HEAD_DIM), lg[hh] * n, F32))
        so_ref[hh] = s_ref[hh] * decay_all + writes[hh]
        o_ref[:, cols[hh]] = _head_norm_gate(
            outs[hh], g_ref[:, cols[hh]], gate_ref[:, cols[hh]]).astype(BF16)


def _ret_sample(log_gamma, p3, g_r, state):
    b, n, _ = p3.shape
    seg = lambda sidx: pl.BlockSpec((None, n, GROUP_WIDTH), lambda b_, lg: (b_, 0, sidx))
    state_spec = pl.BlockSpec((None, N_HEADS, HEAD_DIM, HEAD_DIM), lambda b_, lg: (b_, 0, 0, 0))
    return pl.pallas_call(
        _ret_sample_kernel,
        grid_spec=pltpu.PrefetchScalarGridSpec(
            num_scalar_prefetch=1,
            grid=(b,),
            in_specs=[seg(4), seg(5), seg(6), seg(7),
                      pl.BlockSpec((1, GROUP_WIDTH), lambda b_, lg: (0, 0)),
                      state_spec],
            out_specs=[pl.BlockSpec((None, n, GROUP_WIDTH), lambda b_, lg: (b_, 0, 0)), state_spec],
        ),
        out_shape=[
            jax.ShapeDtypeStruct((b, n, GROUP_WIDTH), BF16),
            jax.ShapeDtypeStruct(state.shape, F32),
        ],
        compiler_params=_params(1),
        name="ret_sample",
    )(log_gamma, p3, p3, p3, p3, g_r, state)


OUT_TM = 512


def _out_proj_kernel(ysb_ref, yr_ref, w_ref, g_ref, x_ref, o_ref):
    out = jnp.dot(ysb_ref[...], w_ref[0:GROUP_WIDTH, :], preferred_element_type=F32)
    out = out + jnp.dot(yr_ref[...], w_ref[GROUP_WIDTH:2 * GROUP_WIDTH, :], preferred_element_type=F32)
    r = lax.rsqrt(jnp.mean(out * out, axis=-1, keepdims=True) + EPS)
    o_ref[...] = x_ref[...] + out * r * g_ref[...]


def _out_proj(y_sb, y_r, w_bf16, g_post, x2d):
    m, d = x2d.shape
    tm = min(OUT_TM, m)
    row = lambda width: pl.BlockSpec((tm, width), lambda i: (i, 0))
    return pl.pallas_call(
        _out_proj_kernel,
        grid=(m // tm,),
        in_specs=[row(GROUP_WIDTH), row(GROUP_WIDTH),
                  pl.BlockSpec(w_bf16.shape, lambda i: (0, 0)),
                  pl.BlockSpec((1, d), lambda i: (0, 0)),
                  row(d)],
        out_specs=row(d),
        out_shape=jax.ShapeDtypeStruct((m, d), F32),
        compiler_params=_params(1),
        name="out_proj",
    )(y_sb, y_r, w_bf16, g_post, x2d)


def _epilogue_tables(positions):
    half = HEAD_DIM // 2
    inv = ROPE_BASE ** (-np.arange(half, dtype=np.float64) / half)
    ang = positions.astype(np.float64)[:, None] * inv[None, :]
    cos = np.concatenate([np.cos(ang), np.cos(ang)], axis=-1)
    sin = np.concatenate([-np.sin(ang), np.sin(ang)], axis=-1)
    one, zero = np.ones_like(cos), np.zeros_like(sin)
    cos4 = np.stack([one * Q_SCALE, one, cos, cos * Q_SCALE])
    sin4 = np.stack([zero, zero, sin, sin * Q_SCALE])
    return jnp.asarray(cos4, F32), jnp.asarray(sin4, F32)


def kernel(x_prompt, x_sample, cache_sb_k, cache_sb_v, state_ret, norm_pre, w_in, sb_head_norm,
           ret_head_norm, w_out, norm_post):
    depth = norm_pre.shape[0]
    assert depth == 1
    b_p, t_p, d = x_prompt.shape
    b_s, t_s, _ = x_sample.shape
    past = cache_sb_k.shape[2]
    log_gamma = jnp.asarray(np.log(1.0 - 2.0 ** (-5.0 - np.arange(N_HEADS))), F32)

    l = 0
    w_in_b = w_in[l].astype(BF16)
    w_out_b = w_out[l].astype(BF16)
    g_pre = norm_pre[l][None, :]
    g_post = norm_post[l][None, :]
    g_sb = sb_head_norm[l][None, :]
    g_r = ret_head_norm[l][None, :]

    cos_p, sin_p = _epilogue_tables(np.arange(t_p))
    xp2 = x_prompt.reshape(b_p * t_p, d)
    p_p, k_p, v_p = _in_proj(xp2, g_pre, w_in_b, cos_p, sin_p)
    p3 = p_p.reshape(b_p, t_p, N_SEGMENTS * GROUP_WIDTH)
    y_sb = _sb_prompt(p3, g_sb)
    y_r, s_p = _ret_prompt(log_gamma, p3, g_r)
    y_p = _out_proj(y_sb.reshape(b_p * t_p, GROUP_WIDTH), y_r.reshape(b_p * t_p, GROUP_WIDTH),
                    w_out_b, g_post, xp2).reshape(b_p, t_p, d)

    cos_s, sin_s = _epilogue_tables(np.tile(past + np.arange(t_s), b_s))
    xs2 = x_sample.reshape(b_s * t_s, d)
    p_s, k_s, v_s = _in_proj(xs2, g_pre, w_in_b, cos_s, sin_s)
    ps3 = p_s.reshape(b_s, t_s, N_SEGMENTS * GROUP_WIDTH)
    ys_sb = _sb_sample(ps3, g_sb, cache_sb_k[l].reshape(b_s, past * N_HEADS, HEAD_DIM),
                       cache_sb_v[l].reshape(b_s, past * N_HEADS, HEAD_DIM))
    ys_r, s_s = _ret_sample(log_gamma, ps3, g_r, state_ret[l])
    y_s = _out_proj(ys_sb.reshape(b_s * t_s, GROUP_WIDTH), ys_r.reshape(b_s * t_s, GROUP_WIDTH),
                    w_out_b, g_post, xs2).reshape(b_s, t_s, d)

    heads = lambda a, bb, tt: a.reshape(1, bb, tt, N_HEADS, HEAD_DIM)
    return (y_p, y_s, heads(k_p, b_p, t_p), heads(v_p, b_p, t_p), s_p[None],
            heads(k_s, b_s, t_s), heads(v_s, b_s, t_s), s_s[None])
```

```python
import functools

import numpy as np
import jax
import jax.numpy as jnp
from jax import lax
from jax.experimental import pallas as pl
from jax.experimental.pallas import tpu as pltpu

HEAD_DIM = 128
N_HEADS = 8
GROUP_WIDTH = N_HEADS * HEAD_DIM
N_SEGMENTS = 8
CHUNK = 64
ROPE_BASE = 10000.0
EPS = 1e-6
Q_SCALE = HEAD_DIM ** -0.5

F32 = jnp.float32
BF16 = jnp.bfloat16

_NT = (((1,), (1,)), ((), ()))
_TN = (((0,), (0,)), ((), ()))

VMEM_LIMIT = 56 * 1024 * 1024


def _params(n_axes):
    return pltpu.CompilerParams(
        dimension_semantics=("arbitrary",) * n_axes, vmem_limit_bytes=VMEM_LIMIT)


def _log_sigmoids(z):
    l = jnp.log(1.0 + jnp.exp(-jnp.abs(z)))
    log_beta = jnp.minimum(z, 0.0) - l
    return log_beta, log_beta - z


def _head_norm_gate(o, g, gate):
    y = o * lax.rsqrt(jnp.mean(o * o, axis=-1, keepdims=True) + EPS) * g
    gf = gate.astype(F32)
    return y * (gf * jax.nn.sigmoid(gf))


IN_TM = 256
IN_TN = 512
_SCALE_Q, _IDENTITY, _ROTARY, _ROTARY_SCALED = range(4)
_TABLE_OF_SEGMENT = (_SCALE_Q, _IDENTITY, _IDENTITY, _IDENTITY, _ROTARY, _ROTARY_SCALED, _IDENTITY, _IDENTITY)
_K_SEGMENT, _V_SEGMENT = 1, 2


def _in_proj_kernel(x_ref, g_ref, w_ref, cos_ref, sin_ref, p_ref, k_ref, v_ref):
    x = x_ref[...]
    r = lax.rsqrt(jnp.mean(x * x, axis=-1, keepdims=True) + EPS)
    h = (x * r * g_ref[...]).astype(BF16)
    for seg, table in enumerate(_TABLE_OF_SEGMENT):
        for c0 in range(0, GROUP_WIDTH, IN_TN):
            col0 = seg * GROUP_WIDTH + c0
            acc = jnp.dot(h, w_ref[:, col0:col0 + IN_TN], preferred_element_type=F32)
            for hh in range(IN_TN // HEAD_DIM):
                xs = acc[:, hh * HEAD_DIM:(hh + 1) * HEAD_DIM]
                y = xs * cos_ref[table]
                if table in (_ROTARY, _ROTARY_SCALED):
                    y = y + pltpu.roll(xs, HEAD_DIM // 2, 1) * sin_ref[table]
                p_ref[:, col0 + hh * HEAD_DIM:col0 + (hh + 1) * HEAD_DIM] = y.astype(BF16)
                head_rows = pl.ds(c0 // HEAD_DIM + hh, IN_TM, stride=N_HEADS)
                if seg == _K_SEGMENT:
                    k_ref[head_rows, :] = xs
                if seg == _V_SEGMENT:
                    v_ref[head_rows, :] = xs


def _in_proj(x2d, g_pre, w_bf16, cos_t, sin_t):
    m, d = x2d.shape
    n = w_bf16.shape[1]
    t_blocks = cos_t.shape[1] // IN_TM
    table_spec = pl.BlockSpec((len(cos_t), IN_TM, HEAD_DIM), lambda i: (0, i % t_blocks, 0))
    return pl.pallas_call(
        _in_proj_kernel,
        grid=(m // IN_TM,),
        in_specs=[
            pl.BlockSpec((IN_TM, d), lambda i: (i, 0)),
            pl.BlockSpec((1, d), lambda i: (0, 0)),
            pl.BlockSpec((d, n), lambda i: (0, 0), pipeline_mode=pl.Buffered(1)),
            table_spec,
            table_spec,
        ],
        out_specs=[
            pl.BlockSpec((IN_TM, n), lambda i: (i, 0)),
            pl.BlockSpec((IN_TM * N_HEADS, HEAD_DIM), lambda i: (i, 0)),
            pl.BlockSpec((IN_TM * N_HEADS, HEAD_DIM), lambda i: (i, 0)),
        ],
        out_shape=[
            jax.ShapeDtypeStruct((m, n), BF16),
            jax.ShapeDtypeStruct((m * N_HEADS, HEAD_DIM), F32),
            jax.ShapeDtypeStruct((m * N_HEADS, HEAD_DIM), F32),
        ],
        compiler_params=_params(1),
        name="in_proj",
    )(x2d, g_pre, w_bf16, cos_t, sin_t)


SB_T = 256
SB_HEADS_PER_STEP = 8
EXP_UNDERFLOW = -104.0
SWITCH_OFF = -1e30


def _sb_prompt_kernel(q_ref, k_ref, v_ref, gate_ref, g_ref, o_ref, vt_ref):
    i = pl.program_id(2)
    n_kb = vt_ref.shape[0]
    heads = range(SB_HEADS_PER_STEP)
    lanes = lambda hh: slice(hh * HEAD_DIM, (hh + 1) * HEAD_DIM)

    @pl.when(i == 0)
    def _():
        for kb in range(n_kb):
            vt_ref[kb] = v_ref[kb * SB_T:(kb + 1) * SB_T, :].astype(F32).T.astype(BF16)

    row = lax.broadcasted_iota(jnp.int32, (SB_T, SB_T), 0)
    col = lax.broadcasted_iota(jnp.int32, (SB_T, SB_T), 1)
    later_keys = (col > row).astype(BF16)
    strict = row < col

    def sweep(blocks, carries, accs):
        zs, log_betas, keeps, laters = [], [], [], []
        for kb, _, _ in blocks:
            start = pl.multiple_of(kb * SB_T, SB_T)
            zs.append([lax.dot_general(k_ref[pl.ds(start, SB_T), lanes(hh)], q_ref[:, lanes(hh)], _NT,
                                       preferred_element_type=F32) for hh in heads])
        for (_, diagonal, _), z_heads in zip(blocks, zs):
            lb_heads, keep_heads = [], []
            for z in z_heads:
                log_beta, log_keep = _log_sigmoids(z)
                if diagonal:
                    log_keep = jnp.where(strict, log_keep, 0.0)
                lb_heads.append(log_beta)
                keep_heads.append(log_keep.astype(BF16))
            log_betas.append(lb_heads)
            keeps.append(keep_heads)
            laters.append([jnp.dot(later_keys, keep, preferred_element_type=F32) for keep in keep_heads])
        for (kb, diagonal, offset), lb_heads, keep_heads, later_heads in zip(blocks, log_betas, keeps, laters):
            if offset is not None:
                carries = [c + offset for c in carries]
            probs = []
            for hh in heads:
                a = jnp.exp(lb_heads[hh] + later_heads[hh] + carries[hh])
                if diagonal:
                    a = jnp.where(strict, a, 0.0)
                probs.append(a.astype(BF16))
            accs = [accs[hh] + jnp.dot(vt_ref[kb, lanes(hh), :], probs[hh], preferred_element_type=F32)
                    for hh in heads]
            carries = [carries[hh] + later_heads[hh][0:1, :] + keep_heads[hh][0:1, :].astype(F32)
                       for hh in heads]
        return carries, accs

    def alive(carries):
        return functools.reduce(jnp.maximum, [jnp.max(c) for c in carries]) > EXP_UNDERFLOW

    previous_off = jnp.where(i > 0, 0.0, SWITCH_OFF).astype(F32)
    carries, accs = sweep([(i, True, None), (jnp.maximum(i - 1, 0), False, previous_off)],
                          [jnp.zeros((1, SB_T), F32) for _ in heads],
                          [jnp.zeros((HEAD_DIM, SB_T), F32) for _ in heads])

    def cond(c):
        return (c[0] < i) & c[1]

    def body(c):
        s, _, carries, accs = c
        carries, accs = sweep([(i - 1 - s, False, None)], carries, accs)
        return s + 1, alive(carries), carries, accs

    _, _, _, accs = lax.while_loop(cond, body, (jnp.int32(1), alive(carries), carries, accs))
    for hh in heads:
        acc = accs[hh]
        normed = (acc * lax.rsqrt(jnp.mean(acc * acc, axis=0, keepdims=True) + EPS)).T
        gate = gate_ref[:, lanes(hh)].astype(F32)
        o_ref[:, lanes(hh)] = (normed * g_ref[:, lanes(hh)] * (gate * jax.nn.sigmoid(gate))).astype(BF16)


def _sb_prompt(p3, g_sb):
    b, t, _ = p3.shape
    w = SB_HEADS_PER_STEP * HEAD_DIM
    sb = GROUP_WIDTH // w
    return pl.pallas_call(
        _sb_prompt_kernel,
        grid=(b, sb, t // SB_T),
        in_specs=[
            pl.BlockSpec((None, SB_T, w), lambda b_, h, i: (b_, i, h)),
            pl.BlockSpec((None, t, w), lambda b_, h, i: (b_, 0, sb + h)),
            pl.BlockSpec((None, t, w), lambda b_, h, i: (b_, 0, 2 * sb + h)),
            pl.BlockSpec((None, SB_T, w), lambda b_, h, i: (b_, i, 3 * sb + h)),
            pl.BlockSpec((1, w), lambda b_, h, i: (0, h)),
        ],
        out_specs=pl.BlockSpec((None, SB_T, w), lambda b_, h, i: (b_, i, h)),
        out_shape=jax.ShapeDtypeStruct((b, t, GROUP_WIDTH), BF16),
        scratch_shapes=[pltpu.VMEM((t // SB_T, w, SB_T), BF16)],
        compiler_params=_params(3),
        name="sb_prompt",
    )(p3, p3, p3, p3, g_sb)


RET_L = 256


RET_HEADS_PER_STEP = 8


def _ret_prompt_kernel(lg_ref, q_ref, k_ref, v_ref, gate_ref, g_ref, o_ref, s_ref,
                       w_ref, read_ref, write_ref):
    group = pl.program_id(0)
    t = q_ref.shape[0]
    heads = range(RET_HEADS_PER_STEP)
    lanes = lambda hh: slice(hh * HEAD_DIM, (hh + 1) * HEAD_DIM)
    log_gamma = [lg_ref[group * RET_HEADS_PER_STEP + hh] for hh in heads]

    @pl.when(pl.program_id(1) == 0)
    def _():
        ii = lax.broadcasted_iota(jnp.int32, (RET_L, RET_L), 0)
        jj = lax.broadcasted_iota(jnp.int32, (RET_L, RET_L), 1)
        dist = jnp.abs(ii - jj).astype(F32)
        visible = jj // CHUNK <= ii // CHUNK
        idx = lax.broadcasted_iota(jnp.int32, (RET_L, HEAD_DIM), 0).astype(F32)
        for hh in heads:
            w_ref[hh] = jnp.where(visible, jnp.exp(log_gamma[hh] * dist), 0.0)
            read_ref[hh] = jnp.exp(log_gamma[hh] * (idx + 1.0))
            write_ref[hh] = jnp.exp(log_gamma[hh] * (RET_L - 1.0 - idx))

    decay_block = [jnp.exp(jnp.full((1, HEAD_DIM), log_gamma[hh] * RET_L, F32)) for hh in heads]

    def body(blk, states):
        rows = pl.ds(pl.multiple_of(blk * RET_L, RET_L), RET_L)
        qs = [q_ref[rows, lanes(hh)] for hh in heads]
        ks = [k_ref[rows, lanes(hh)] for hh in heads]
        vs = [v_ref[rows, lanes(hh)] for hh in heads]
        scores = [lax.dot_general(qs[hh], ks[hh], _NT, preferred_element_type=F32) for hh in heads]
        reads = [jnp.dot(qs[hh], states[hh].astype(BF16), preferred_element_type=F32) for hh in heads]
        kws = [(ks[hh].astype(F32) * write_ref[hh]).astype(BF16) for hh in heads]
        writes = [lax.dot_general(kws[hh], vs[hh], _TN, preferred_element_type=F32) for hh in heads]
        weighted = [(scores[hh] * w_ref[hh]).astype(BF16) for hh in heads]
        outs = [jnp.dot(weighted[hh], vs[hh], preferred_element_type=F32) + reads[hh] * read_ref[hh]
                for hh in heads]
        for hh in heads:
            o_ref[rows, lanes(hh)] = _head_norm_gate(
                outs[hh], g_ref[:, lanes(hh)], gate_ref[rows, lanes(hh)]).astype(BF16)
        return [states[hh] * decay_block[hh] + writes[hh] for hh in heads]

    states = lax.fori_loop(0, t // RET_L, body, [jnp.zeros((HEAD_DIM, HEAD_DIM), F32) for _ in heads])
    for hh in heads:
        s_ref[hh] = states[hh]


def _ret_prompt(log_gamma, p3, g_r):
    b, t, _ = p3.shape
    w = RET_HEADS_PER_STEP * HEAD_DIM
    groups = GROUP_WIDTH // w
    col = lambda seg: (lambda g_, b_, lg: (b_, 0, seg * groups + g_))
    return pl.pallas_call(
        _ret_prompt_kernel,
        grid_spec=pltpu.PrefetchScalarGridSpec(
            num_scalar_prefetch=1,
            grid=(groups, b),
            in_specs=[
                pl.BlockSpec((None, t, w), col(4)),
                pl.BlockSpec((None, t, w), col(5)),
                pl.BlockSpec((None, t, w), col(6)),
                pl.BlockSpec((None, t, w), col(7)),
                pl.BlockSpec((1, w), lambda g_, b_, lg: (0, g_)),
            ],
            out_specs=[
                pl.BlockSpec((None, t, w), lambda g_, b_, lg: (b_, 0, g_)),
                pl.BlockSpec((None, RET_HEADS_PER_STEP, HEAD_DIM, HEAD_DIM),
                             lambda g_, b_, lg: (b_, g_, 0, 0)),
            ],
            scratch_shapes=[
                pltpu.VMEM((RET_HEADS_PER_STEP, RET_L, RET_L), F32),
                pltpu.VMEM((RET_HEADS_PER_STEP, RET_L, HEAD_DIM), F32),
                pltpu.VMEM((RET_HEADS_PER_STEP, RET_L, HEAD_DIM), F32),
            ],
        ),
        out_shape=[
            jax.ShapeDtypeStruct((b, t, GROUP_WIDTH), BF16),
            jax.ShapeDtypeStruct((b, N_HEADS, HEAD_DIM, HEAD_DIM), F32),
        ],
        compiler_params=_params(2),
        name="ret_prompt",
    )(log_gamma, p3, p3, p3, p3, g_r)


SBS_TK = 256
SBS_STREAMS = 4


def _sb_sample_kernel(q_ref, kn_ref, vn_ref, gate_ref, g_ref, kc_hbm, vc_hbm, o_ref,
                      qbd_ref, kbuf, vbuf, sems):
    step = pl.program_id(0)
    streams = range(SBS_STREAMS)
    n = q_ref.shape[1]
    lanes = N_HEADS * n
    block_rows = SBS_TK * N_HEADS
    n_blocks = kc_hbm.shape[1] // block_rows
    spare = 2 * SBS_STREAMS

    def copies(stream, blk, slot):
        rows = pl.ds((n_blocks - 1 - blk) * block_rows, block_rows)
        return (pltpu.make_async_copy(kc_hbm.at[stream, rows], kbuf.at[slot], sems.at[0, slot]),
                pltpu.make_async_copy(vc_hbm.at[stream, rows], vbuf.at[slot], sems.at[1, slot]))

    def start(pair):
        pair[0].start()
        pair[1].start()

    def wait(pair):
        pair[0].wait()
        pair[1].wait()

    def first_slot(at_step, st):
        return (at_step % 2) * SBS_STREAMS + st

    @pl.when(step == 0)
    def _():
        for st in streams:
            start(copies(st, 0, first_slot(0, st)))

    @pl.when(step + 1 < pl.num_programs(0))
    def _():
        for st in streams:
            start(copies((step + 1) * SBS_STREAMS + st, 0, first_slot(step + 1, st)))

    rr = lax.broadcasted_iota(jnp.int32, (lanes, GROUP_WIDTH), 0) // n
    cc = lax.broadcasted_iota(jnp.int32, (lanes, GROUP_WIDTH), 1) // HEAD_DIM
    for st in streams:
        qt = jnp.concatenate([q_ref[st]] * N_HEADS, axis=0)
        qbd_ref[st] = jnp.where(rr == cc, qt, jnp.zeros_like(qt))

    ki = lax.broadcasted_iota(jnp.int32, (n, lanes), 0)
    qi = lax.broadcasted_iota(jnp.int32, (n, lanes), 1) % n
    strict = ki < qi
    mj = lax.broadcasted_iota(jnp.int32, (n, n), 0)
    mm = lax.broadcasted_iota(jnp.int32, (n, n), 1)
    later_new = (mm > mj).astype(F32)
    carries, accs = [], []
    for st in streams:
        z = lax.dot_general(kn_ref[st], qbd_ref[st], _NT, preferred_element_type=F32)
        log_beta, log_keep = _log_sigmoids(z)
        lk = jnp.where(strict, log_keep, 0.0)
        later = jnp.dot(later_new, lk, preferred_element_type=F32)
        a = jnp.where(strict, jnp.exp(log_beta + later), 0.0)
        accs.append(lax.dot_general(a.astype(BF16), vn_ref[st], _TN, preferred_element_type=F32))
        carries.append(jnp.sum(lk, axis=0, keepdims=True))

    kj = lax.broadcasted_iota(jnp.int32, (SBS_TK, SBS_TK), 0)
    km = lax.broadcasted_iota(jnp.int32, (SBS_TK, SBS_TK), 1)
    later_keys = (km > kj).astype(BF16)

    def all_heads(buf, slot):
        return jnp.concatenate(
            [buf[slot, pl.ds(hh, SBS_TK, stride=N_HEADS), :].astype(BF16) for hh in range(N_HEADS)],
            axis=1)

    def cached_blocks(which, slots, carries, accs):
        zs = [lax.dot_general(all_heads(kbuf, slot), qbd_ref[st], _NT, preferred_element_type=F32)
              for st, slot in zip(which, slots)]
        sig = [_log_sigmoids(z) for z in zs]
        lks = [log_keep.astype(BF16) for _, log_keep in sig]
        laters = [jnp.dot(later_keys, lk, preferred_element_type=F32) for lk in lks]
        probs = [jnp.exp(log_beta + later + carry).astype(BF16)
                 for (log_beta, _), later, carry in zip(sig, laters, carries)]
        accs = [acc + lax.dot_general(a, all_heads(vbuf, slot), _TN, preferred_element_type=F32)
                for acc, a, slot in zip(accs, probs, slots)]
        carries = [carry + later[0:1, :] + lk[0:1, :].astype(F32)
                   for carry, later, lk in zip(carries, laters, lks)]
        return carries, accs

    first_slots = [first_slot(step, st) for st in streams]
    for st in streams:
        wait(copies(step * SBS_STREAMS + st, 0, first_slots[st]))
    carries, accs = cached_blocks(list(streams), first_slots, carries, accs)

    for st in streams:
        def cond(c):
            return (c[0] < n_blocks) & c[1]

        def body(c, st=st):
            blk, _, carry, acc = c
            pair = copies(step * SBS_STREAMS + st, blk, spare)
            start(pair)
            wait(pair)
            (carry,), (acc,) = cached_blocks([st], [spare], [carry], [acc])
            return blk + 1, jnp.max(carry) > EXP_UNDERFLOW, carry, acc

        _, _, _, acc = lax.while_loop(
            cond, body, (jnp.int32(1), jnp.max(carries[st]) > EXP_UNDERFLOW, carries[st], accs[st]))

        for hh in range(N_HEADS):
            cols = slice(hh * HEAD_DIM, (hh + 1) * HEAD_DIM)
            o = acc[hh * n:(hh + 1) * n, cols]
            o_ref[st, :, cols] = _head_norm_gate(o, g_ref[:, cols], gate_ref[st, :, cols]).astype(BF16)


def _sb_sample(p3, g_sb, cache_k, cache_v):
    b, n, _ = p3.shape
    seg = lambda sidx: pl.BlockSpec((SBS_STREAMS, n, GROUP_WIDTH), lambda s: (s, 0, sidx))
    slots = 2 * SBS_STREAMS + 1
    return pl.pallas_call(
        _sb_sample_kernel,
        grid=(b // SBS_STREAMS,),
        in_specs=[seg(0), seg(1), seg(2), seg(3),
                  pl.BlockSpec((1, GROUP_WIDTH), lambda s: (0, 0)),
                  pl.BlockSpec(memory_space=pl.ANY),
                  pl.BlockSpec(memory_space=pl.ANY)],
        out_specs=pl.BlockSpec((SBS_STREAMS, n, GROUP_WIDTH), lambda s: (s, 0, 0)),
        out_shape=jax.ShapeDtypeStruct((b, n, GROUP_WIDTH), BF16),
        scratch_shapes=[
            pltpu.VMEM((SBS_STREAMS, N_HEADS * n, GROUP_WIDTH), BF16),
            pltpu.VMEM((slots, SBS_TK * N_HEADS, HEAD_DIM), F32),
            pltpu.VMEM((slots, SBS_TK * N_HEADS, HEAD_DIM), F32),
            pltpu.SemaphoreType.DMA((2, slots)),
        ],
        compiler_params=_params(1),
        name="sb_sample",
    )(p3, p3, p3, p3, g_sb, cache_k, cache_v)


RET_SAMPLE_STREAMS = 4


def _ret_sample_kernel(lg_ref, q_ref, k_ref, v_ref, gate_ref, g_ref, s_ref, o_ref, so_ref):
    n = q_ref.shape[1]
    ii = lax.broadcasted_iota(jnp.int32, (n, n), 0)
    jj = lax.broadcasted_iota(jnp.int32, (n, n), 1)
    dist = jnp.abs(ii - jj).astype(F32)
    idx = lax.broadcasted_iota(jnp.int32, (n, 1), 0).astype(F32)
    cols = [slice(hh * HEAD_DIM, (hh + 1) * HEAD_DIM) for hh in range(N_HEADS)]
    lg = [lg_ref[hh] for hh in range(N_HEADS)]
    intra = [jnp.exp(lg[hh] * dist) for hh in range(N_HEADS)]
    decay_read = [jnp.exp(lg[hh] * (idx + 1.0)) for hh in range(N_HEADS)]
    decay_write = [jnp.exp(lg[hh] * (n - 1.0 - idx)) for hh in range(N_HEADS)]
    decay_all = [jnp.exp(jnp.full((1, HEAD_DIM), lg[hh] * n, F32)) for hh in range(N_HEADS)]
    pairs = [(st, hh) for st in range(q_ref.shape[0]) for hh in range(N_HEADS)]
    qs = [q_ref[st, :, cols[hh]] for st, hh in pairs]
    ks = [k_ref[st, :, cols[hh]] for st, hh in pairs]
    vs = [v_ref[st, :, cols[hh]] for st, hh in pairs]
    scores = [lax.dot_general(q, k, _NT, preferred_element_type=F32) for q, k in zip(qs, ks)]
    reads = [jnp.dot(q, s_ref[st, hh].astype(BF16), preferred_element_type=F32)
             for q, (st, hh) in zip(qs, pairs)]
    kws = [(k.astype(F32) * decay_write[hh]).astype(BF16) for k, (_, hh) in zip(ks, pairs)]
    writes = [lax.dot_general(kw, v, _TN, preferred_element_type=F32) for kw, v in zip(kws, vs)]
    weighted = [(sc * intra[hh]).astype(BF16) for sc, (_, hh) in zip(scores, pairs)]
    outs = [jnp.dot(w, v, preferred_element_type=F32) + rd * decay_read[hh]
            for w, v, rd, (_, hh) in zip(weighted, vs, reads, pairs)]
    for (st, hh), out, write in zip(pairs, outs, writes):
        so_ref[st, hh] = s_ref[st, hh] * decay_all[hh] + write
        o_ref[st, :, cols[hh]] = _head_norm_gate(
            out, g_ref[:, cols[hh]], gate_ref[st, :, cols[hh]]).astype(BF16)


def _ret_sample(log_gamma, p3, g_r, state):
    b, n, _ = p3.shape
    rs = RET_SAMPLE_STREAMS
    seg = lambda sidx: pl.BlockSpec((rs, n, GROUP_WIDTH), lambda b_, lg: (b_, 0, sidx))
    state_spec = pl.BlockSpec((rs, N_HEADS, HEAD_DIM, HEAD_DIM), lambda b_, lg: (b_, 0, 0, 0))
    return pl.pallas_call(
        _ret_sample_kernel,
        grid_spec=pltpu.PrefetchScalarGridSpec(
            num_scalar_prefetch=1,
            grid=(b // rs,),
            in_specs=[seg(4), seg(5), seg(6), seg(7),
                      pl.BlockSpec((1, GROUP_WIDTH), lambda b_, lg: (0, 0)),
                      state_spec],
            out_specs=[pl.BlockSpec((rs, n, GROUP_WIDTH), lambda b_, lg: (b_, 0, 0)), state_spec],
        ),
        out_shape=[
            jax.ShapeDtypeStruct((b, n, GROUP_WIDTH), BF16),
            jax.ShapeDtypeStruct(state.shape, F32),
        ],
        compiler_params=_params(1),
        name="ret_sample",
    )(log_gamma, p3, p3, p3, p3, g_r, state)


OUT_TM = 512


def _out_proj_kernel(ysb_ref, yr_ref, w_ref, g_ref, x_ref, o_ref):
    out = jnp.dot(ysb_ref[...], w_ref[0:GROUP_WIDTH, :], preferred_element_type=F32)
    out = out + jnp.dot(yr_ref[...], w_ref[GROUP_WIDTH:2 * GROUP_WIDTH, :], preferred_element_type=F32)
    r = lax.rsqrt(jnp.mean(out * out, axis=-1, keepdims=True) + EPS)
    o_ref[...] = x_ref[...] + out * r * g_ref[...]


def _out_proj(y_sb, y_r, w_bf16, g_post, x2d):
    m, d = x2d.shape
    tm = min(OUT_TM, m)
    row = lambda width: pl.BlockSpec((tm, width), lambda i: (i, 0))
    return pl.pallas_call(
        _out_proj_kernel,
        grid=(m // tm,),
        in_specs=[row(GROUP_WIDTH), row(GROUP_WIDTH),
                  pl.BlockSpec(w_bf16.shape, lambda i: (0, 0)),
                  pl.BlockSpec((1, d), lambda i: (0, 0)),
                  row(d)],
        out_specs=row(d),
        out_shape=jax.ShapeDtypeStruct((m, d), F32),
        compiler_params=_params(1),
        name="out_proj",
    )(y_sb, y_r, w_bf16, g_post, x2d)


def _epilogue_tables(positions):
    half = HEAD_DIM // 2
    inv = ROPE_BASE ** (-np.arange(half, dtype=np.float64) / half)
    ang = positions.astype(np.float64)[:, None] * inv[None, :]
    cos = np.concatenate([np.cos(ang), np.cos(ang)], axis=-1)
    sin = np.concatenate([-np.sin(ang), np.sin(ang)], axis=-1)
    one, zero = np.ones_like(cos), np.zeros_like(sin)
    cos4 = np.stack([one * Q_SCALE, one, cos, cos * Q_SCALE])
    sin4 = np.stack([zero, zero, sin, sin * Q_SCALE])
    return jnp.asarray(cos4, F32), jnp.asarray(sin4, F32)


def kernel(x_prompt, x_sample, cache_sb_k, cache_sb_v, state_ret, norm_pre, w_in, sb_head_norm,
           ret_head_norm, w_out, norm_post):
    depth = norm_pre.shape[0]
    assert depth == 1
    b_p, t_p, d = x_prompt.shape
    b_s, t_s, _ = x_sample.shape
    past = cache_sb_k.shape[2]
    log_gamma = jnp.asarray(np.log(1.0 - 2.0 ** (-5.0 - np.arange(N_HEADS))), F32)

    l = 0
    w_in_b = w_in[l].astype(BF16)
    w_out_b = w_out[l].astype(BF16)
    g_pre = norm_pre[l][None, :]
    g_post = norm_post[l][None, :]
    g_sb = sb_head_norm[l][None, :]
    g_r = ret_head_norm[l][None, :]

    cos_p, sin_p = _epilogue_tables(np.arange(t_p))
    xp2 = x_prompt.reshape(b_p * t_p, d)
    p_p, k_p, v_p = _in_proj(xp2, g_pre, w_in_b, cos_p, sin_p)
    p3 = p_p.reshape(b_p, t_p, N_SEGMENTS * GROUP_WIDTH)
    y_sb = _sb_prompt(p3, g_sb)
    y_r, s_p = _ret_prompt(log_gamma, p3, g_r)
    y_p = _out_proj(y_sb.reshape(b_p * t_p, GROUP_WIDTH), y_r.reshape(b_p * t_p, GROUP_WIDTH),
                    w_out_b, g_post, xp2).reshape(b_p, t_p, d)

    cos_s, sin_s = _epilogue_tables(np.tile(past + np.arange(t_s), b_s))
    xs2 = x_sample.reshape(b_s * t_s, d)
    p_s, k_s, v_s = _in_proj(xs2, g_pre, w_in_b, cos_s, sin_s)
    ps3 = p_s.reshape(b_s, t_s, N_SEGMENTS * GROUP_WIDTH)
    ys_sb = _sb_sample(ps3, g_sb, cache_sb_k[l].reshape(b_s, past * N_HEADS, HEAD_DIM),
                       cache_sb_v[l].reshape(b_s, past * N_HEADS, HEAD_DIM))
    ys_r, s_s = _ret_sample(log_gamma, ps3, g_r, state_ret[l])
    y_s = _out_proj(ys_sb.reshape(b_s * t_s, GROUP_WIDTH), ys_r.reshape(b_s * t_s, GROUP_WIDTH),
                    w_out_b, g_post, xs2).reshape(b_s, t_s, d)

    heads = lambda a, bb, tt: a.reshape(1, bb, tt, N_HEADS, HEAD_DIM)
    return (y_p, y_s, heads(k_p, b_p, t_p), heads(v_p, b_p, t_p), s_p[None],
            heads(k_s, b_s, t_s), heads(v_s, b_s, t_s), s_s[None])
```

```python
import functools

import numpy as np
import jax
import jax.numpy as jnp
from jax import lax
from jax.experimental import pallas as pl
from jax.experimental.pallas import tpu as pltpu

HEAD_DIM = 128
N_HEADS = 8
GROUP_WIDTH = N_HEADS * HEAD_DIM
N_SEGMENTS = 8
CHUNK = 64
ROPE_BASE = 10000.0
EPS = 1e-6
Q_SCALE = HEAD_DIM ** -0.5

F32 = jnp.float32
BF16 = jnp.bfloat16

_NT = (((1,), (1,)), ((), ()))
_TN = (((0,), (0,)), ((), ()))

VMEM_LIMIT = 56 * 1024 * 1024


def _params(n_axes):
    return pltpu.CompilerParams(
        dimension_semantics=("arbitrary",) * n_axes, vmem_limit_bytes=VMEM_LIMIT)


def _log_sigmoids(z):
    l = jnp.log(1.0 + jnp.exp(-jnp.abs(z)))
    log_beta = jnp.minimum(z, 0.0) - l
    return log_beta, log_beta - z


def _head_norm_gate(o, g, gate):
    y = o * lax.rsqrt(jnp.mean(o * o, axis=-1, keepdims=True) + EPS) * g
    gf = gate.astype(F32)
    return y * (gf * jax.nn.sigmoid(gf))


IN_TM = 256
IN_TN = 512
_SCALE_Q, _IDENTITY, _ROTARY, _ROTARY_SCALED = range(4)
_TABLE_OF_SEGMENT = (_SCALE_Q, _IDENTITY, _IDENTITY, _IDENTITY, _ROTARY, _ROTARY_SCALED, _IDENTITY, _IDENTITY)
_K_SEGMENT, _V_SEGMENT = 1, 2


def _in_proj_kernel(x_ref, g_ref, w_ref, cos_ref, sin_ref, p_ref, k_ref, v_ref):
    x = x_ref[...]
    r = lax.rsqrt(jnp.mean(x * x, axis=-1, keepdims=True) + EPS)
    h = (x * r * g_ref[...]).astype(BF16)
    for seg, table in enumerate(_TABLE_OF_SEGMENT):
        for c0 in range(0, GROUP_WIDTH, IN_TN):
            col0 = seg * GROUP_WIDTH + c0
            acc = jnp.dot(h, w_ref[:, col0:col0 + IN_TN], preferred_element_type=F32)
            for hh in range(IN_TN // HEAD_DIM):
                xs = acc[:, hh * HEAD_DIM:(hh + 1) * HEAD_DIM]
                y = xs * cos_ref[table]
                if table in (_ROTARY, _ROTARY_SCALED):
                    y = y + pltpu.roll(xs, HEAD_DIM // 2, 1) * sin_ref[table]
                p_ref[:, col0 + hh * HEAD_DIM:col0 + (hh + 1) * HEAD_DIM] = y.astype(BF16)
                head_rows = pl.ds(c0 // HEAD_DIM + hh, IN_TM, stride=N_HEADS)
                if seg == _K_SEGMENT:
                    k_ref[head_rows, :] = xs
                if seg == _V_SEGMENT:
                    v_ref[head_rows, :] = xs


def _in_proj(x2d, g_pre, w_bf16, cos_t, sin_t):
    m, d = x2d.shape
    n = w_bf16.shape[1]
    t_blocks = cos_t.shape[1] // IN_TM
    table_spec = pl.BlockSpec((len(cos_t), IN_TM, HEAD_DIM), lambda i: (0, i % t_blocks, 0))
    return pl.pallas_call(
        _in_proj_kernel,
        grid=(m // IN_TM,),
        in_specs=[
            pl.BlockSpec((IN_TM, d), lambda i: (i, 0)),
            pl.BlockSpec((1, d), lambda i: (0, 0)),
            pl.BlockSpec((d, n), lambda i: (0, 0), pipeline_mode=pl.Buffered(1)),
            table_spec,
            table_spec,
        ],
        out_specs=[
            pl.BlockSpec((IN_TM, n), lambda i: (i, 0)),
            pl.BlockSpec((IN_TM * N_HEADS, HEAD_DIM), lambda i: (i, 0)),
            pl.BlockSpec((IN_TM * N_HEADS, HEAD_DIM), lambda i: (i, 0)),
        ],
        out_shape=[
            jax.ShapeDtypeStruct((m, n), BF16),
            jax.ShapeDtypeStruct((m * N_HEADS, HEAD_DIM), F32),
            jax.ShapeDtypeStruct((m * N_HEADS, HEAD_DIM), F32),
        ],
        compiler_params=_params(1),
        name="in_proj",
    )(x2d, g_pre, w_bf16, cos_t, sin_t)


SB_T = 256
SB_HEADS_PER_STEP = 8
SB_QBLOCKS_PER_STEP = 2
EXP_UNDERFLOW = -104.0
SWITCH_OFF = -1e30


def _sb_prompt_kernel(q_ref, k_ref, v_ref, gate_ref, g_ref, o_ref, vt_ref):
    i = pl.program_id(2)
    n_kb = vt_ref.shape[0]
    heads = range(SB_HEADS_PER_STEP)
    qblocks = range(SB_QBLOCKS_PER_STEP)
    lanes = lambda hh: slice(hh * HEAD_DIM, (hh + 1) * HEAD_DIM)
    q_rows = lambda qq: slice(qq * SB_T, (qq + 1) * SB_T)

    @pl.when(i == 0)
    def _():
        for kb in range(n_kb):
            vt_ref[kb] = v_ref[kb * SB_T:(kb + 1) * SB_T, :].astype(F32).T.astype(BF16)

    row = lax.broadcasted_iota(jnp.int32, (SB_T, SB_T), 0)
    col = lax.broadcasted_iota(jnp.int32, (SB_T, SB_T), 1)
    later_keys = (col > row).astype(BF16)
    strict = row < col

    def sweep(jobs, carries, accs):
        zs, log_betas, keeps, laters = [], [], [], []
        for qq, kb, _, _ in jobs:
            start = pl.multiple_of(kb * SB_T, SB_T)
            zs.append([lax.dot_general(k_ref[pl.ds(start, SB_T), lanes(hh)], q_ref[q_rows(qq), lanes(hh)],
                                       _NT, preferred_element_type=F32) for hh in heads])
        for (_, _, diagonal, _), z_heads in zip(jobs, zs):
            lb_heads, keep_heads = [], []
            for z in z_heads:
                log_beta, log_keep = _log_sigmoids(z)
                if diagonal:
                    log_keep = jnp.where(strict, log_keep, 0.0)
                lb_heads.append(log_beta)
                keep_heads.append(log_keep.astype(BF16))
            log_betas.append(lb_heads)
            keeps.append(keep_heads)
            laters.append([jnp.dot(later_keys, keep, preferred_element_type=F32) for keep in keep_heads])
        carries = [list(c) for c in carries]
        accs = [list(a) for a in accs]
        for (qq, kb, diagonal, offset), lb_heads, keep_heads, later_heads in zip(jobs, log_betas, keeps, laters):
            if offset is not None:
                carries[qq] = [c + offset for c in carries[qq]]
            probs = []
            for hh in heads:
                a = jnp.exp(lb_heads[hh] + later_heads[hh] + carries[qq][hh])
                if diagonal:
                    a = jnp.where(strict, a, 0.0)
                probs.append(a.astype(BF16))
            accs[qq] = [accs[qq][hh] + jnp.dot(vt_ref[kb, lanes(hh), :], probs[hh],
                                                preferred_element_type=F32) for hh in heads]
            carries[qq] = [carries[qq][hh] + later_heads[hh][0:1, :] + keep_heads[hh][0:1, :].astype(F32)
                           for hh in heads]
        return carries, accs

    def alive(carries):
        return functools.reduce(jnp.maximum, [jnp.max(c) for cq in carries for c in cq]) > EXP_UNDERFLOW

    first = [i * SB_QBLOCKS_PER_STEP + qq for qq in qblocks]

    def older(qq, back):
        kb = first[qq] - back
        return (qq, jnp.maximum(kb, 0), False, jnp.where(kb >= 0, 0.0, SWITCH_OFF).astype(F32))

    carries, accs = sweep(
        [(qq, first[qq], True, None) for qq in qblocks] + [older(qq, 1) for qq in qblocks],
        [[jnp.zeros((1, SB_T), F32) for _ in heads] for _ in qblocks],
        [[jnp.zeros((HEAD_DIM, SB_T), F32) for _ in heads] for _ in qblocks])

    def cond(c):
        return (c[0] <= first[-1]) & c[1]

    def body(c):
        back, _, carries, accs = c
        carries, accs = sweep([older(qq, back) for qq in qblocks], carries, accs)
        return back + 1, alive(carries), carries, accs

    _, _, _, accs = lax.while_loop(cond, body, (jnp.int32(2), alive(carries), carries, accs))
    for qq in qblocks:
        for hh in heads:
            acc = accs[qq][hh]
            normed = (acc * lax.rsqrt(jnp.mean(acc * acc, axis=0, keepdims=True) + EPS)).T
            gate = gate_ref[q_rows(qq), lanes(hh)].astype(F32)
            o_ref[q_rows(qq), lanes(hh)] = (
                normed * g_ref[:, lanes(hh)] * (gate * jax.nn.sigmoid(gate))).astype(BF16)


def _sb_prompt(p3, g_sb):
    b, t, _ = p3.shape
    w = SB_HEADS_PER_STEP * HEAD_DIM
    sb = GROUP_WIDTH // w
    tq = SB_QBLOCKS_PER_STEP * SB_T
    return pl.pallas_call(
        _sb_prompt_kernel,
        grid=(b, sb, t // tq),
        in_specs=[
            pl.BlockSpec((None, tq, w), lambda b_, h, i: (b_, i, h)),
            pl.BlockSpec((None, t, w), lambda b_, h, i: (b_, 0, sb + h)),
            pl.BlockSpec((None, t, w), lambda b_, h, i: (b_, 0, 2 * sb + h)),
            pl.BlockSpec((None, tq, w), lambda b_, h, i: (b_, i, 3 * sb + h)),
            pl.BlockSpec((1, w), lambda b_, h, i: (0, h)),
        ],
        out_specs=pl.BlockSpec((None, tq, w), lambda b_, h, i: (b_, i, h)),
        out_shape=jax.ShapeDtypeStruct((b, t, GROUP_WIDTH), BF16),
        scratch_shapes=[pltpu.VMEM((t // SB_T, w, SB_T), BF16)],
        compiler_params=_params(3),
        name="sb_prompt",
    )(p3, p3, p3, p3, g_sb)


RET_L = 256


RET_HEADS_PER_STEP = 8


def _ret_prompt_kernel(lg_ref, q_ref, k_ref, v_ref, gate_ref, g_ref, o_ref, s_ref,
                       w_ref, read_ref, write_ref):
    group = pl.program_id(0)
    t = q_ref.shape[0]
    heads = range(RET_HEADS_PER_STEP)
    lanes = lambda hh: slice(hh * HEAD_DIM, (hh + 1) * HEAD_DIM)
    log_gamma = [lg_ref[group * RET_HEADS_PER_STEP + hh] for hh in heads]

    @pl.when(pl.program_id(1) == 0)
    def _():
        ii = lax.broadcasted_iota(jnp.int32, (RET_L, RET_L), 0)
        jj = lax.broadcasted_iota(jnp.int32, (RET_L, RET_L), 1)
        dist = jnp.abs(ii - jj).astype(F32)
        visible = jj // CHUNK <= ii // CHUNK
        idx = lax.broadcasted_iota(jnp.int32, (RET_L, HEAD_DIM), 0).astype(F32)
        for hh in heads:
            w_ref[hh] = jnp.where(visible, jnp.exp(log_gamma[hh] * dist), 0.0)
            read_ref[hh] = jnp.exp(log_gamma[hh] * (idx + 1.0))
            write_ref[hh] = jnp.exp(log_gamma[hh] * (RET_L - 1.0 - idx))

    decay_block = [jnp.exp(jnp.full((1, HEAD_DIM), log_gamma[hh] * RET_L, F32)) for hh in heads]

    def body(blk, states):
        rows = pl.ds(pl.multiple_of(blk * RET_L, RET_L), RET_L)
        qs = [q_ref[rows, lanes(hh)] for hh in heads]
        ks = [k_ref[rows, lanes(hh)] for hh in heads]
        vs = [v_ref[rows, lanes(hh)] for hh in heads]
        scores = [lax.dot_general(qs[hh], ks[hh], _NT, preferred_element_type=F32) for hh in heads]
        reads = [jnp.dot(qs[hh], states[hh].astype(BF16), preferred_element_type=F32) for hh in heads]
        kws = [(ks[hh].astype(F32) * write_ref[hh]).astype(BF16) for hh in heads]
        writes = [lax.dot_general(kws[hh], vs[hh], _TN, preferred_element_type=F32) for hh in heads]
        weighted = [(scores[hh] * w_ref[hh]).astype(BF16) for hh in heads]
        outs = [jnp.dot(weighted[hh], vs[hh], preferred_element_type=F32) + reads[hh] * read_ref[hh]
                for hh in heads]
        for hh in heads:
            o_ref[rows, lanes(hh)] = _head_norm_gate(
                outs[hh], g_ref[:, lanes(hh)], gate_ref[rows, lanes(hh)]).astype(BF16)
        return [states[hh] * decay_block[hh] + writes[hh] for hh in heads]

    states = lax.fori_loop(0, t // RET_L, body, [jnp.zeros((HEAD_DIM, HEAD_DIM), F32) for _ in heads])
    for hh in heads:
        s_ref[hh] = states[hh]


def _ret_prompt(log_gamma, p3, g_r):
    b, t, _ = p3.shape
    w = RET_HEADS_PER_STEP * HEAD_DIM
    groups = GROUP_WIDTH // w
    col = lambda seg: (lambda g_, b_, lg: (b_, 0, seg * groups + g_))
    return pl.pallas_call(
        _ret_prompt_kernel,
        grid_spec=pltpu.PrefetchScalarGridSpec(
            num_scalar_prefetch=1,
            grid=(groups, b),
            in_specs=[
                pl.BlockSpec((None, t, w), col(4)),
                pl.BlockSpec((None, t, w), col(5)),
                pl.BlockSpec((None, t, w), col(6)),
                pl.BlockSpec((None, t, w), col(7)),
                pl.BlockSpec((1, w), lambda g_, b_, lg: (0, g_)),
            ],
            out_specs=[
                pl.BlockSpec((None, t, w), lambda g_, b_, lg: (b_, 0, g_)),
                pl.BlockSpec((None, RET_HEADS_PER_STEP, HEAD_DIM, HEAD_DIM),
                             lambda g_, b_, lg: (b_, g_, 0, 0)),
            ],
            scratch_shapes=[
                pltpu.VMEM((RET_HEADS_PER_STEP, RET_L, RET_L), F32),
                pltpu.VMEM((RET_HEADS_PER_STEP, RET_L, HEAD_DIM), F32),
                pltpu.VMEM((RET_HEADS_PER_STEP, RET_L, HEAD_DIM), F32),
            ],
        ),
        out_shape=[
            jax.ShapeDtypeStruct((b, t, GROUP_WIDTH), BF16),
            jax.ShapeDtypeStruct((b, N_HEADS, HEAD_DIM, HEAD_DIM), F32),
        ],
        compiler_params=_params(2),
        name="ret_prompt",
    )(log_gamma, p3, p3, p3, p3, g_r)


SBS_TK = 256
SBS_STREAMS = 4


def _sb_sample_kernel(q_ref, kn_ref, vn_ref, gate_ref, g_ref, kc_hbm, vc_hbm, o_ref,
                      qbd_ref, kbuf, vbuf, sems):
    step = pl.program_id(0)
    streams = range(SBS_STREAMS)
    n = q_ref.shape[1]
    lanes = N_HEADS * n
    block_rows = SBS_TK * N_HEADS
    n_blocks = kc_hbm.shape[1] // block_rows
    spare = 2 * SBS_STREAMS

    def copies(stream, blk, slot):
        rows = pl.ds((n_blocks - 1 - blk) * block_rows, block_rows)
        return (pltpu.make_async_copy(kc_hbm.at[stream, rows], kbuf.at[slot], sems.at[0, slot]),
                pltpu.make_async_copy(vc_hbm.at[stream, rows], vbuf.at[slot], sems.at[1, slot]))

    def start(pair):
        pair[0].start()
        pair[1].start()

    def wait(pair):
        pair[0].wait()
        pair[1].wait()

    def first_slot(at_step, st):
        return (at_step % 2) * SBS_STREAMS + st

    @pl.when(step == 0)
    def _():
        for st in streams:
            start(copies(st, 0, first_slot(0, st)))

    @pl.when(step + 1 < pl.num_programs(0))
    def _():
        for st in streams:
            start(copies((step + 1) * SBS_STREAMS + st, 0, first_slot(step + 1, st)))

    rr = lax.broadcasted_iota(jnp.int32, (lanes, GROUP_WIDTH), 0) // n
    cc = lax.broadcasted_iota(jnp.int32, (lanes, GROUP_WIDTH), 1) // HEAD_DIM
    for st in streams:
        qt = jnp.concatenate([q_ref[st]] * N_HEADS, axis=0)
        qbd_ref[st] = jnp.where(rr == cc, qt, jnp.zeros_like(qt))

    ki = lax.broadcasted_iota(jnp.int32, (n, lanes), 0)
    qi = lax.broadcasted_iota(jnp.int32, (n, lanes), 1) % n
    strict = ki < qi
    mj = lax.broadcasted_iota(jnp.int32, (n, n), 0)
    mm = lax.broadcasted_iota(jnp.int32, (n, n), 1)
    later_new = (mm > mj).astype(F32)
    carries, accs = [], []
    for st in streams:
        z = lax.dot_general(kn_ref[st], qbd_ref[st], _NT, preferred_element_type=F32)
        log_beta, log_keep = _log_sigmoids(z)
        lk = jnp.where(strict, log_keep, 0.0)
        later = jnp.dot(later_new, lk, preferred_element_type=F32)
        a = jnp.where(strict, jnp.exp(log_beta + later), 0.0)
        accs.append(lax.dot_general(a.astype(BF16), vn_ref[st], _TN, preferred_element_type=F32))
        carries.append(jnp.sum(lk, axis=0, keepdims=True))

    kj = lax.broadcasted_iota(jnp.int32, (SBS_TK, SBS_TK), 0)
    km = lax.broadcasted_iota(jnp.int32, (SBS_TK, SBS_TK), 1)
    later_keys = (km > kj).astype(BF16)

    def all_heads(buf, slot):
        return jnp.concatenate(
            [buf[slot, pl.ds(hh, SBS_TK, stride=N_HEADS), :].astype(BF16) for hh in range(N_HEADS)],
            axis=1)

    def cached_blocks(which, slots, carries, accs):
        zs = [lax.dot_general(all_heads(kbuf, slot), qbd_ref[st], _NT, preferred_element_type=F32)
              for st, slot in zip(which, slots)]
        sig = [_log_sigmoids(z) for z in zs]
        lks = [log_keep.astype(BF16) for _, log_keep in sig]
        laters = [jnp.dot(later_keys, lk, preferred_element_type=F32) for lk in lks]
        probs = [jnp.exp(log_beta + later + carry).astype(BF16)
                 for (log_beta, _), later, carry in zip(sig, laters, carries)]
        accs = [acc + lax.dot_general(a, all_heads(vbuf, slot), _TN, preferred_element_type=F32)
                for acc, a, slot in zip(accs, probs, slots)]
        carries = [carry + later[0:1, :] + lk[0:1, :].astype(F32)
                   for carry, later, lk in zip(carries, laters, lks)]
        return carries, accs

    first_slots = [first_slot(step, st) for st in streams]
    for st in streams:
        wait(copies(step * SBS_STREAMS + st, 0, first_slots[st]))
    carries, accs = cached_blocks(list(streams), first_slots, carries, accs)

    for st in streams:
        def cond(c):
            return (c[0] < n_blocks) & c[1]

        def body(c, st=st):
            blk, _, carry, acc = c
            pair = copies(step * SBS_STREAMS + st, blk, spare)
            start(pair)
            wait(pair)
            (carry,), (acc,) = cached_blocks([st], [spare], [carry], [acc])
            return blk + 1, jnp.max(carry) > EXP_UNDERFLOW, carry, acc

        _, _, _, acc = lax.while_loop(
            cond, body, (jnp.int32(1), jnp.max(carries[st]) > EXP_UNDERFLOW, carries[st], accs[st]))

        for hh in range(N_HEADS):
            cols = slice(hh * HEAD_DIM, (hh + 1) * HEAD_DIM)
            o = acc[hh * n:(hh + 1) * n, cols]
            o_ref[st, :, cols] = _head_norm_gate(o, g_ref[:, cols], gate_ref[st, :, cols]).astype(BF16)


def _sb_sample(p3, g_sb, cache_k, cache_v):
    b, n, _ = p3.shape
    seg = lambda sidx: pl.BlockSpec((SBS_STREAMS, n, GROUP_WIDTH), lambda s: (s, 0, sidx))
    slots = 2 * SBS_STREAMS + 1
    return pl.pallas_call(
        _sb_sample_kernel,
        grid=(b // SBS_STREAMS,),
        in_specs=[seg(0), seg(1), seg(2), seg(3),
                  pl.BlockSpec((1, GROUP_WIDTH), lambda s: (0, 0)),
                  pl.BlockSpec(memory_space=pl.ANY),
                  pl.BlockSpec(memory_space=pl.ANY)],
        out_specs=pl.BlockSpec((SBS_STREAMS, n, GROUP_WIDTH), lambda s: (s, 0, 0)),
        out_shape=jax.ShapeDtypeStruct((b, n, GROUP_WIDTH), BF16),
        scratch_shapes=[
            pltpu.VMEM((SBS_STREAMS, N_HEADS * n, GROUP_WIDTH), BF16),
            pltpu.VMEM((slots, SBS_TK * N_HEADS, HEAD_DIM), F32),
            pltpu.VMEM((slots, SBS_TK * N_HEADS, HEAD_DIM), F32),
            pltpu.SemaphoreType.DMA((2, slots)),
        ],
        compiler_params=_params(1),
        name="sb_sample",
    )(p3, p3, p3, p3, g_sb, cache_k, cache_v)


RET_SAMPLE_STREAMS = 4


def _ret_sample_kernel(lg_ref, q_ref, k_ref, v_ref, gate_ref, g_ref, s_ref, o_ref, so_ref):
    n = q_ref.shape[1]
    ii = lax.broadcasted_iota(jnp.int32, (n, n), 0)
    jj = lax.broadcasted_iota(jnp.int32, (n, n), 1)
    dist = jnp.abs(ii - jj).astype(F32)
    idx = lax.broadcasted_iota(jnp.int32, (n, 1), 0).astype(F32)
    cols = [slice(hh * HEAD_DIM, (hh + 1) * HEAD_DIM) for hh in range(N_HEADS)]
    lg = [lg_ref[hh] for hh in range(N_HEADS)]
    intra = [jnp.exp(lg[hh] * dist) for hh in range(N_HEADS)]
    decay_read = [jnp.exp(lg[hh] * (idx + 1.0)) for hh in range(N_HEADS)]
    decay_write = [jnp.exp(lg[hh] * (n - 1.0 - idx)) for hh in range(N_HEADS)]
    decay_all = [jnp.exp(jnp.full((1, HEAD_DIM), lg[hh] * n, F32)) for hh in range(N_HEADS)]
    pairs = [(st, hh) for st in range(q_ref.shape[0]) for hh in range(N_HEADS)]
    qs = [q_ref[st, :, cols[hh]] for st, hh in pairs]
    ks = [k_ref[st, :, cols[hh]] for st, hh in pairs]
    vs = [v_ref[st, :, cols[hh]] for st, hh in pairs]
    scores = [lax.dot_general(q, k, _NT, preferred_element_type=F32) for q, k in zip(qs, ks)]
    reads = [jnp.dot(q, s_ref[st, hh].astype(BF16), preferred_element_type=F32)
             for q, (st, hh) in zip(qs, pairs)]
    kws = [(k.astype(F32) * decay_write[hh]).astype(BF16) for k, (_, hh) in zip(ks, pairs)]
    writes = [lax.dot_general(kw, v, _TN, preferred_element_type=F32) for kw, v in zip(kws, vs)]
    weighted = [(sc * intra[hh]).astype(BF16) for sc, (_, hh) in zip(scores, pairs)]
    outs = [jnp.dot(w, v, preferred_element_type=F32) + rd * decay_read[hh]
            for w, v, rd, (_, hh) in zip(weighted, vs, reads, pairs)]
    for (st, hh), out, write in zip(pairs, outs, writes):
        so_ref[st, hh] = s_ref[st, hh] * decay_all[hh] + write
        o_ref[st, :, cols[hh]] = _head_norm_gate(
            out, g_ref[:, cols[hh]], gate_ref[st, :, cols[hh]]).astype(BF16)


def _ret_sample(log_gamma, p3, g_r, state):
    b, n, _ = p3.shape
    rs = RET_SAMPLE_STREAMS
    seg = lambda sidx: pl.BlockSpec((rs, n, GROUP_WIDTH), lambda b_, lg: (b_, 0, sidx))
    state_spec = pl.BlockSpec((rs, N_HEADS, HEAD_DIM, HEAD_DIM), lambda b_, lg: (b_, 0, 0, 0))
    return pl.pallas_call(
        _ret_sample_kernel,
        grid_spec=pltpu.PrefetchScalarGridSpec(
            num_scalar_prefetch=1,
            grid=(b // rs,),
            in_specs=[seg(4), seg(5), seg(6), seg(7),
                      pl.BlockSpec((1, GROUP_WIDTH), lambda b_, lg: (0, 0)),
                      state_spec],
            out_specs=[pl.BlockSpec((rs, n, GROUP_WIDTH), lambda b_, lg: (b_, 0, 0)), state_spec],
        ),
        out_shape=[
            jax.ShapeDtypeStruct((b, n, GROUP_WIDTH), BF16),
            jax.ShapeDtypeStruct(state.shape, F32),
        ],
        compiler_params=_params(1),
        name="ret_sample",
    )(log_gamma, p3, p3, p3, p3, g_r, state)


OUT_TM = 512


def _out_proj_kernel(ysb_ref, yr_ref, w_ref, g_ref, x_ref, o_ref):
    out = jnp.dot(ysb_ref[...], w_ref[0:GROUP_WIDTH, :], preferred_element_type=F32)
    out = out + jnp.dot(yr_ref[...], w_ref[GROUP_WIDTH:2 * GROUP_WIDTH, :], preferred_element_type=F32)
    r = lax.rsqrt(jnp.mean(out * out, axis=-1, keepdims=True) + EPS)
    o_ref[...] = x_ref[...] + out * r * g_ref[...]


def _out_proj(y_sb, y_r, w_bf16, g_post, x2d):
    m, d = x2d.shape
    tm = min(OUT_TM, m)
    row = lambda width: pl.BlockSpec((tm, width), lambda i: (i, 0))
    return pl.pallas_call(
        _out_proj_kernel,
        grid=(m // tm,),
        in_specs=[row(GROUP_WIDTH), row(GROUP_WIDTH),
                  pl.BlockSpec(w_bf16.shape, lambda i: (0, 0)),
                  pl.BlockSpec((1, d), lambda i: (0, 0)),
                  row(d)],
        out_specs=row(d),
        out_shape=jax.ShapeDtypeStruct((m, d), F32),
        compiler_params=_params(1),
        name="out_proj",
    )(y_sb, y_r, w_bf16, g_post, x2d)


def _epilogue_tables(positions):
    half = HEAD_DIM // 2
    inv = ROPE_BASE ** (-np.arange(half, dtype=np.float64) / half)
    ang = positions.astype(np.float64)[:, None] * inv[None, :]
    cos = np.concatenate([np.cos(ang), np.cos(ang)], axis=-1)
    sin = np.concatenate([-np.sin(ang), np.sin(ang)], axis=-1)
    one, zero = np.ones_like(cos), np.zeros_like(sin)
    cos4 = np.stack([one * Q_SCALE, one, cos, cos * Q_SCALE])
    sin4 = np.stack([zero, zero, sin, sin * Q_SCALE])
    return jnp.asarray(cos4, F32), jnp.asarray(sin4, F32)


def kernel(x_prompt, x_sample, cache_sb_k, cache_sb_v, state_ret, norm_pre, w_in, sb_head_norm,
           ret_head_norm, w_out, norm_post):
    depth = norm_pre.shape[0]
    assert depth == 1
    b_p, t_p, d = x_prompt.shape
    b_s, t_s, _ = x_sample.shape
    past = cache_sb_k.shape[2]
    log_gamma = jnp.asarray(np.log(1.0 - 2.0 ** (-5.0 - np.arange(N_HEADS))), F32)

    l = 0
    w_in_b = w_in[l].astype(BF16)
    w_out_b = w_out[l].astype(BF16)
    g_pre = norm_pre[l][None, :]
    g_post = norm_post[l][None, :]
    g_sb = sb_head_norm[l][None, :]
    g_r = ret_head_norm[l][None, :]

    cos_p, sin_p = _epilogue_tables(np.arange(t_p))
    xp2 = x_prompt.reshape(b_p * t_p, d)
    p_p, k_p, v_p = _in_proj(xp2, g_pre, w_in_b, cos_p, sin_p)
    p3 = p_p.reshape(b_p, t_p, N_SEGMENTS * GROUP_WIDTH)
    y_sb = _sb_prompt(p3, g_sb)
    y_r, s_p = _ret_prompt(log_gamma, p3, g_r)
    y_p = _out_proj(y_sb.reshape(b_p * t_p, GROUP_WIDTH), y_r.reshape(b_p * t_p, GROUP_WIDTH),
                    w_out_b, g_post, xp2).reshape(b_p, t_p, d)

    cos_s, sin_s = _epilogue_tables(np.tile(past + np.arange(t_s), b_s))
    xs2 = x_sample.reshape(b_s * t_s, d)
    p_s, k_s, v_s = _in_proj(xs2, g_pre, w_in_b, cos_s, sin_s)
    ps3 = p_s.reshape(b_s, t_s, N_SEGMENTS * GROUP_WIDTH)
    ys_sb = _sb_sample(ps3, g_sb, cache_sb_k[l].reshape(b_s, past * N_HEADS, HEAD_DIM),
                       cache_sb_v[l].reshape(b_s, past * N_HEADS, HEAD_DIM))
    ys_r, s_s = _ret_sample(log_gamma, ps3, g_r, state_ret[l])
    y_s = _out_proj(ys_sb.reshape(b_s * t_s, GROUP_WIDTH), ys_r.reshape(b_s * t_s, GROUP_WIDTH),
                    w_out_b, g_post, xs2).reshape(b_s, t_s, d)

    heads = lambda a, bb, tt: a.reshape(1, bb, tt, N_HEADS, HEAD_DIM)
    return (y_p, y_s, heads(k_p, b_p, t_p), heads(v_p, b_p, t_p), s_p[None],
            heads(k_s, b_s, t_s), heads(v_s, b_s, t_s), s_s[None])
```

```python
import functools

import numpy as np
import jax
import jax.numpy as jnp
from jax import lax
from jax.experimental import pallas as pl
from jax.experimental.pallas import tpu as pltpu

HEAD_DIM = 128
N_HEADS = 8
GROUP_WIDTH = N_HEADS * HEAD_DIM
N_SEGMENTS = 8
CHUNK = 64
ROPE_BASE = 10000.0
EPS = 1e-6
Q_SCALE = HEAD_DIM ** -0.5

F32 = jnp.float32
BF16 = jnp.bfloat16

_NT = (((1,), (1,)), ((), ()))
_TN = (((0,), (0,)), ((), ()))

VMEM_LIMIT = 56 * 1024 * 1024


def _params(n_axes):
    return pltpu.CompilerParams(
        dimension_semantics=("arbitrary",) * n_axes, vmem_limit_bytes=VMEM_LIMIT)


def _log_sigmoids(z):
    l = jnp.log(1.0 + jnp.exp(-jnp.abs(z)))
    log_beta = jnp.minimum(z, 0.0) - l
    return log_beta, log_beta - z


def _head_norm_gate(o, g, gate):
    y = o * lax.rsqrt(jnp.mean(o * o, axis=-1, keepdims=True) + EPS) * g
    gf = gate.astype(F32)
    return y * (gf * jax.nn.sigmoid(gf))


IN_TM = 256
IN_TN = 512
_SCALE_Q, _IDENTITY, _ROTARY, _ROTARY_SCALED = range(4)
_TABLE_OF_SEGMENT = (_SCALE_Q, _IDENTITY, _IDENTITY, _IDENTITY, _ROTARY, _ROTARY_SCALED, _IDENTITY, _IDENTITY)
_K_SEGMENT, _V_SEGMENT = 1, 2


def _in_proj_kernel(x_ref, g_ref, w_ref, cos_ref, sin_ref, p_ref, k_ref, v_ref):
    x = x_ref[...]
    r = lax.rsqrt(jnp.mean(x * x, axis=-1, keepdims=True) + EPS)
    h = (x * r * g_ref[...]).astype(BF16)
    for seg, table in enumerate(_TABLE_OF_SEGMENT):
        for c0 in range(0, GROUP_WIDTH, IN_TN):
            col0 = seg * GROUP_WIDTH + c0
            acc = jnp.dot(h, w_ref[:, col0:col0 + IN_TN], preferred_element_type=F32)
            for hh in range(IN_TN // HEAD_DIM):
                xs = acc[:, hh * HEAD_DIM:(hh + 1) * HEAD_DIM]
                y = xs * cos_ref[table]
                if table in (_ROTARY, _ROTARY_SCALED):
                    y = y + pltpu.roll(xs, HEAD_DIM // 2, 1) * sin_ref[table]
                p_ref[:, col0 + hh * HEAD_DIM:col0 + (hh + 1) * HEAD_DIM] = y.astype(BF16)
                head_rows = pl.ds(c0 // HEAD_DIM + hh, IN_TM, stride=N_HEADS)
                if seg == _K_SEGMENT:
                    k_ref[head_rows, :] = xs
                if seg == _V_SEGMENT:
                    v_ref[head_rows, :] = xs


def _in_proj(x2d, g_pre, w_bf16, cos_t, sin_t):
    m, d = x2d.shape
    n = w_bf16.shape[1]
    t_blocks = cos_t.shape[1] // IN_TM
    table_spec = pl.BlockSpec((len(cos_t), IN_TM, HEAD_DIM), lambda i: (0, i % t_blocks, 0))
    return pl.pallas_call(
        _in_proj_kernel,
        grid=(m // IN_TM,),
        in_specs=[
            pl.BlockSpec((IN_TM, d), lambda i: (i, 0)),
            pl.BlockSpec((1, d), lambda i: (0, 0)),
            pl.BlockSpec((d, n), lambda i: (0, 0), pipeline_mode=pl.Buffered(1)),
            table_spec,
            table_spec,
        ],
        out_specs=[
            pl.BlockSpec((IN_TM, n), lambda i: (i, 0)),
            pl.BlockSpec((IN_TM * N_HEADS, HEAD_DIM), lambda i: (i, 0)),
            pl.BlockSpec((IN_TM * N_HEADS, HEAD_DIM), lambda i: (i, 0)),
        ],
        out_shape=[
            jax.ShapeDtypeStruct((m, n), BF16),
            jax.ShapeDtypeStruct((m * N_HEADS, HEAD_DIM), F32),
            jax.ShapeDtypeStruct((m * N_HEADS, HEAD_DIM), F32),
        ],
        compiler_params=_params(1),
        name="in_proj",
    )(x2d, g_pre, w_bf16, cos_t, sin_t)


SB_T = 256
SB_HEADS_PER_STEP = 8
SB_QBLOCKS_PER_STEP = 2
EXP_UNDERFLOW = -104.0
SWITCH_OFF = -1e30


def _sb_prompt_kernel(q_ref, k_ref, v_ref, gate_ref, g_ref, o_ref, vt_ref):
    i = pl.program_id(2)
    n_kb = vt_ref.shape[0]
    heads = range(SB_HEADS_PER_STEP)
    qblocks = range(SB_QBLOCKS_PER_STEP)
    lanes = lambda hh: slice(hh * HEAD_DIM, (hh + 1) * HEAD_DIM)
    q_rows = lambda qq: slice(qq * SB_T, (qq + 1) * SB_T)

    @pl.when(i == 0)
    def _():
        for kb in range(n_kb):
            vt_ref[kb] = v_ref[kb * SB_T:(kb + 1) * SB_T, :].astype(F32).T.astype(BF16)

    row = lax.broadcasted_iota(jnp.int32, (SB_T, SB_T), 0)
    col = lax.broadcasted_iota(jnp.int32, (SB_T, SB_T), 1)
    later_keys = (col > row).astype(BF16)
    strict = row < col

    def sweep(jobs, carries, accs):
        zs, log_betas, keeps, laters = [], [], [], []
        for qq, kb, _, _ in jobs:
            start = pl.multiple_of(kb * SB_T, SB_T)
            zs.append([lax.dot_general(k_ref[pl.ds(start, SB_T), lanes(hh)], q_ref[q_rows(qq), lanes(hh)],
                                       _NT, preferred_element_type=F32) for hh in heads])
        for (_, _, diagonal, _), z_heads in zip(jobs, zs):
            lb_heads, keep_heads = [], []
            for z in z_heads:
                log_beta, log_keep = _log_sigmoids(z)
                if diagonal:
                    log_keep = jnp.where(strict, log_keep, 0.0)
                lb_heads.append(log_beta)
                keep_heads.append(log_keep.astype(BF16))
            log_betas.append(lb_heads)
            keeps.append(keep_heads)
            laters.append([jnp.dot(later_keys, keep, preferred_element_type=F32) for keep in keep_heads])
        carries = [list(c) for c in carries]
        accs = [list(a) for a in accs]
        for (qq, kb, diagonal, offset), lb_heads, keep_heads, later_heads in zip(jobs, log_betas, keeps, laters):
            if offset is not None:
                carries[qq] = [c + offset for c in carries[qq]]
            probs = []
            for hh in heads:
                a = jnp.exp(lb_heads[hh] + later_heads[hh] + carries[qq][hh])
                if diagonal:
                    a = jnp.where(strict, a, 0.0)
                probs.append(a.astype(BF16))
            accs[qq] = [accs[qq][hh] + jnp.dot(vt_ref[kb, lanes(hh), :], probs[hh],
                                                preferred_element_type=F32) for hh in heads]
            carries[qq] = [carries[qq][hh] + later_heads[hh][0:1, :] + keep_heads[hh][0:1, :].astype(F32)
                           for hh in heads]
        return carries, accs

    def alive(carries):
        return functools.reduce(jnp.maximum, [jnp.max(c) for cq in carries for c in cq]) > EXP_UNDERFLOW

    first = [i * SB_QBLOCKS_PER_STEP + qq for qq in qblocks]

    def older(qq, back):
        kb = first[qq] - back
        return (qq, jnp.maximum(kb, 0), False, jnp.where(kb >= 0, 0.0, SWITCH_OFF).astype(F32))

    carries, accs = sweep(
        [(qq, first[qq], True, None) for qq in qblocks] + [older(qq, 1) for qq in qblocks],
        [[jnp.zeros((1, SB_T), F32) for _ in heads] for _ in qblocks],
        [[jnp.zeros((HEAD_DIM, SB_T), F32) for _ in heads] for _ in qblocks])

    def cond(c):
        return (c[0] <= first[-1]) & c[1]

    def body(c):
        back, _, carries, accs = c
        carries, accs = sweep([older(qq, back) for qq in qblocks], carries, accs)
        return back + 1, alive(carries), carries, accs

    _, _, _, accs = lax.while_loop(cond, body, (jnp.int32(2), alive(carries), carries, accs))
    for qq in qblocks:
        for hh in heads:
            acc = accs[qq][hh]
            normed = (acc * lax.rsqrt(jnp.mean(acc * acc, axis=0, keepdims=True) + EPS)).T
            gate = gate_ref[q_rows(qq), lanes(hh)].astype(F32)
            o_ref[q_rows(qq), lanes(hh)] = (
                normed * g_ref[:, lanes(hh)] * (gate * jax.nn.sigmoid(gate))).astype(BF16)


def _sb_prompt(p3, g_sb):
    b, t, _ = p3.shape
    w = SB_HEADS_PER_STEP * HEAD_DIM
    sb = GROUP_WIDTH // w
    tq = SB_QBLOCKS_PER_STEP * SB_T
    return pl.pallas_call(
        _sb_prompt_kernel,
        grid=(b, sb, t // tq),
        in_specs=[
            pl.BlockSpec((None, tq, w), lambda b_, h, i: (b_, i, h)),
            pl.BlockSpec((None, t, w), lambda b_, h, i: (b_, 0, sb + h)),
            pl.BlockSpec((None, t, w), lambda b_, h, i: (b_, 0, 2 * sb + h)),
            pl.BlockSpec((None, tq, w), lambda b_, h, i: (b_, i, 3 * sb + h)),
            pl.BlockSpec((1, w), lambda b_, h, i: (0, h)),
        ],
        out_specs=pl.BlockSpec((None, tq, w), lambda b_, h, i: (b_, i, h)),
        out_shape=jax.ShapeDtypeStruct((b, t, GROUP_WIDTH), BF16),
        scratch_shapes=[pltpu.VMEM((t // SB_T, w, SB_T), BF16)],
        compiler_params=_params(3),
        name="sb_prompt",
    )(p3, p3, p3, p3, g_sb)


RET_L = 256


RET_HEADS_PER_STEP = 8
RET_BLOCKS_PER_ITER = 2


def _ret_prompt_kernel(lg_ref, q_ref, k_ref, v_ref, gate_ref, g_ref, o_ref, s_ref,
                       w_ref, read_ref, write_ref):
    group = pl.program_id(0)
    t = q_ref.shape[0]
    heads = range(RET_HEADS_PER_STEP)
    lanes = lambda hh: slice(hh * HEAD_DIM, (hh + 1) * HEAD_DIM)
    log_gamma = [lg_ref[group * RET_HEADS_PER_STEP + hh] for hh in heads]

    @pl.when(pl.program_id(1) == 0)
    def _():
        ii = lax.broadcasted_iota(jnp.int32, (RET_L, RET_L), 0)
        jj = lax.broadcasted_iota(jnp.int32, (RET_L, RET_L), 1)
        dist = jnp.abs(ii - jj).astype(F32)
        visible = jj // CHUNK <= ii // CHUNK
        idx = lax.broadcasted_iota(jnp.int32, (RET_L, HEAD_DIM), 0).astype(F32)
        for hh in heads:
            w_ref[hh] = jnp.where(visible, jnp.exp(log_gamma[hh] * dist), 0.0)
            read_ref[hh] = jnp.exp(log_gamma[hh] * (idx + 1.0))
            write_ref[hh] = jnp.exp(log_gamma[hh] * (RET_L - 1.0 - idx))

    decay_block = [jnp.exp(jnp.full((1, HEAD_DIM), log_gamma[hh] * RET_L, F32)) for hh in heads]

    def body(it, states):
        subs = range(RET_BLOCKS_PER_ITER)
        rows = [pl.ds(pl.multiple_of((it * RET_BLOCKS_PER_ITER + sub) * RET_L, RET_L), RET_L) for sub in subs]
        qs = [[q_ref[r, lanes(hh)] for hh in heads] for r in rows]
        ks = [[k_ref[r, lanes(hh)] for hh in heads] for r in rows]
        vs = [[v_ref[r, lanes(hh)] for hh in heads] for r in rows]
        scores = [[lax.dot_general(qs[sub][hh], ks[sub][hh], _NT, preferred_element_type=F32)
                   for hh in heads] for sub in subs]
        kws = [[(ks[sub][hh].astype(F32) * write_ref[hh]).astype(BF16) for hh in heads] for sub in subs]
        writes = [[lax.dot_general(kws[sub][hh], vs[sub][hh], _TN, preferred_element_type=F32)
                   for hh in heads] for sub in subs]
        weighted = [[(scores[sub][hh] * w_ref[hh]).astype(BF16) for hh in heads] for sub in subs]
        intra = [[jnp.dot(weighted[sub][hh], vs[sub][hh], preferred_element_type=F32)
                  for hh in heads] for sub in subs]
        for sub in subs:
            reads = [jnp.dot(qs[sub][hh], states[hh].astype(BF16), preferred_element_type=F32)
                     for hh in heads]
            for hh in heads:
                o_ref[rows[sub], lanes(hh)] = _head_norm_gate(
                    intra[sub][hh] + reads[hh] * read_ref[hh], g_ref[:, lanes(hh)],
                    gate_ref[rows[sub], lanes(hh)]).astype(BF16)
            states = [states[hh] * decay_block[hh] + writes[sub][hh] for hh in heads]
        return states

    states = lax.fori_loop(0, t // (RET_L * RET_BLOCKS_PER_ITER), body,
                           [jnp.zeros((HEAD_DIM, HEAD_DIM), F32) for _ in heads])
    for hh in heads:
        s_ref[hh] = states[hh]


def _ret_prompt(log_gamma, p3, g_r):
    b, t, _ = p3.shape
    w = RET_HEADS_PER_STEP * HEAD_DIM
    groups = GROUP_WIDTH // w
    col = lambda seg: (lambda g_, b_, lg: (b_, 0, seg * groups + g_))
    return pl.pallas_call(
        _ret_prompt_kernel,
        grid_spec=pltpu.PrefetchScalarGridSpec(
            num_scalar_prefetch=1,
            grid=(groups, b),
            in_specs=[
                pl.BlockSpec((None, t, w), col(4)),
                pl.BlockSpec((None, t, w), col(5)),
                pl.BlockSpec((None, t, w), col(6)),
                pl.BlockSpec((None, t, w), col(7)),
                pl.BlockSpec((1, w), lambda g_, b_, lg: (0, g_)),
            ],
            out_specs=[
                pl.BlockSpec((None, t, w), lambda g_, b_, lg: (b_, 0, g_)),
                pl.BlockSpec((None, RET_HEADS_PER_STEP, HEAD_DIM, HEAD_DIM),
                             lambda g_, b_, lg: (b_, g_, 0, 0)),
            ],
            scratch_shapes=[
                pltpu.VMEM((RET_HEADS_PER_STEP, RET_L, RET_L), F32),
                pltpu.VMEM((RET_HEADS_PER_STEP, RET_L, HEAD_DIM), F32),
                pltpu.VMEM((RET_HEADS_PER_STEP, RET_L, HEAD_DIM), F32),
            ],
        ),
        out_shape=[
            jax.ShapeDtypeStruct((b, t, GROUP_WIDTH), BF16),
            jax.ShapeDtypeStruct((b, N_HEADS, HEAD_DIM, HEAD_DIM), F32),
        ],
        compiler_params=_params(2),
        name="ret_prompt",
    )(log_gamma, p3, p3, p3, p3, g_r)


SBS_TK = 256
SBS_STREAMS = 4


def _sb_sample_kernel(q_ref, kn_ref, vn_ref, gate_ref, g_ref, kc_hbm, vc_hbm, o_ref,
                      qbd_ref, kbuf, vbuf, sems):
    step = pl.program_id(0)
    streams = range(SBS_STREAMS)
    n = q_ref.shape[1]
    lanes = N_HEADS * n
    block_rows = SBS_TK * N_HEADS
    n_blocks = kc_hbm.shape[1] // block_rows
    spare = 2 * SBS_STREAMS

    def copies(stream, blk, slot):
        rows = pl.ds((n_blocks - 1 - blk) * block_rows, block_rows)
        return (pltpu.make_async_copy(kc_hbm.at[stream, rows], kbuf.at[slot], sems.at[0, slot]),
                pltpu.make_async_copy(vc_hbm.at[stream, rows], vbuf.at[slot], sems.at[1, slot]))

    def start(pair):
        pair[0].start()
        pair[1].start()

    def wait(pair):
        pair[0].wait()
        pair[1].wait()

    def first_slot(at_step, st):
        return (at_step % 2) * SBS_STREAMS + st

    @pl.when(step == 0)
    def _():
        for st in streams:
            start(copies(st, 0, first_slot(0, st)))

    @pl.when(step + 1 < pl.num_programs(0))
    def _():
        for st in streams:
            start(copies((step + 1) * SBS_STREAMS + st, 0, first_slot(step + 1, st)))

    rr = lax.broadcasted_iota(jnp.int32, (lanes, GROUP_WIDTH), 0) // n
    cc = lax.broadcasted_iota(jnp.int32, (lanes, GROUP_WIDTH), 1) // HEAD_DIM
    for st in streams:
        qt = jnp.concatenate([q_ref[st]] * N_HEADS, axis=0)
        qbd_ref[st] = jnp.where(rr == cc, qt, jnp.zeros_like(qt))

    ki = lax.broadcasted_iota(jnp.int32, (n, lanes), 0)
    qi = lax.broadcasted_iota(jnp.int32, (n, lanes), 1) % n
    strict = ki < qi
    mj = lax.broadcasted_iota(jnp.int32, (n, n), 0)
    mm = lax.broadcasted_iota(jnp.int32, (n, n), 1)
    later_new = (mm > mj).astype(F32)
    carries, accs = [], []
    for st in streams:
        z = lax.dot_general(kn_ref[st], qbd_ref[st], _NT, preferred_element_type=F32)
        log_beta, log_keep = _log_sigmoids(z)
        lk = jnp.where(strict, log_keep, 0.0)
        later = jnp.dot(later_new, lk, preferred_element_type=F32)
        a = jnp.where(strict, jnp.exp(log_beta + later), 0.0)
        accs.append(lax.dot_general(a.astype(BF16), vn_ref[st], _TN, preferred_element_type=F32))
        carries.append(jnp.sum(lk, axis=0, keepdims=True))

    kj = lax.broadcasted_iota(jnp.int32, (SBS_TK, SBS_TK), 0)
    km = lax.broadcasted_iota(jnp.int32, (SBS_TK, SBS_TK), 1)
    later_keys = (km > kj).astype(BF16)

    def all_heads(buf, slot):
        return jnp.concatenate(
            [buf[slot, pl.ds(hh, SBS_TK, stride=N_HEADS), :].astype(BF16) for hh in range(N_HEADS)],
            axis=1)

    def cached_blocks(which, slots, carries, accs):
        zs = [lax.dot_general(all_heads(kbuf, slot), qbd_ref[st], _NT, preferred_element_type=F32)
              for st, slot in zip(which, slots)]
        sig = [_log_sigmoids(z) for z in zs]
        lks = [log_keep.astype(BF16) for _, log_keep in sig]
        laters = [jnp.dot(later_keys, lk, preferred_element_type=F32) for lk in lks]
        probs = [jnp.exp(log_beta + later + carry).astype(BF16)
                 for (log_beta, _), later, carry in zip(sig, laters, carries)]
        accs = [acc + lax.dot_general(a, all_heads(vbuf, slot), _TN, preferred_element_type=F32)
                for acc, a, slot in zip(accs, probs, slots)]
        carries = [carry + later[0:1, :] + lk[0:1, :].astype(F32)
                   for carry, later, lk in zip(carries, laters, lks)]
        return carries, accs

    first_slots = [first_slot(step, st) for st in streams]
    for st in streams:
        wait(copies(step * SBS_STREAMS + st, 0, first_slots[st]))
    carries, accs = cached_blocks(list(streams), first_slots, carries, accs)

    for st in streams:
        def cond(c):
            return (c[0] < n_blocks) & c[1]

        def body(c, st=st):
            blk, _, carry, acc = c
            pair = copies(step * SBS_STREAMS + st, blk, spare)
            start(pair)
            wait(pair)
            (carry,), (acc,) = cached_blocks([st], [spare], [carry], [acc])
            return blk + 1, jnp.max(carry) > EXP_UNDERFLOW, carry, acc

        _, _, _, acc = lax.while_loop(
            cond, body, (jnp.int32(1), jnp.max(carries[st]) > EXP_UNDERFLOW, carries[st], accs[st]))

        for hh in range(N_HEADS):
            cols = slice(hh * HEAD_DIM, (hh + 1) * HEAD_DIM)
            o = acc[hh * n:(hh + 1) * n, cols]
            o_ref[st, :, cols] = _head_norm_gate(o, g_ref[:, cols], gate_ref[st, :, cols]).astype(BF16)


def _sb_sample(p3, g_sb, cache_k, cache_v):
    b, n, _ = p3.shape
    seg = lambda sidx: pl.BlockSpec((SBS_STREAMS, n, GROUP_WIDTH), lambda s: (s, 0, sidx))
    slots = 2 * SBS_STREAMS + 1
    return pl.pallas_call(
        _sb_sample_kernel,
        grid=(b // SBS_STREAMS,),
        in_specs=[seg(0), seg(1), seg(2), seg(3),
                  pl.BlockSpec((1, GROUP_WIDTH), lambda s: (0, 0)),
                  pl.BlockSpec(memory_space=pl.ANY),
                  pl.BlockSpec(memory_space=pl.ANY)],
        out_specs=pl.BlockSpec((SBS_STREAMS, n, GROUP_WIDTH), lambda s: (s, 0, 0)),
        out_shape=jax.ShapeDtypeStruct((b, n, GROUP_WIDTH), BF16),
        scratch_shapes=[
            pltpu.VMEM((SBS_STREAMS, N_HEADS * n, GROUP_WIDTH), BF16),
            pltpu.VMEM((slots, SBS_TK * N_HEADS, HEAD_DIM), F32),
            pltpu.VMEM((slots, SBS_TK * N_HEADS, HEAD_DIM), F32),
            pltpu.SemaphoreType.DMA((2, slots)),
        ],
        compiler_params=_params(1),
        name="sb_sample",
    )(p3, p3, p3, p3, g_sb, cache_k, cache_v)


RET_SAMPLE_STREAMS = 4


def _ret_sample_kernel(lg_ref, q_ref, k_ref, v_ref, gate_ref, g_ref, s_ref, o_ref, so_ref):
    n = q_ref.shape[1]
    ii = lax.broadcasted_iota(jnp.int32, (n, n), 0)
    jj = lax.broadcasted_iota(jnp.int32, (n, n), 1)
    dist = jnp.abs(ii - jj).astype(F32)
    idx = lax.broadcasted_iota(jnp.int32, (n, 1), 0).astype(F32)
    cols = [slice(hh * HEAD_DIM, (hh + 1) * HEAD_DIM) for hh in range(N_HEADS)]
    lg = [lg_ref[hh] for hh in range(N_HEADS)]
    intra = [jnp.exp(lg[hh] * dist) for hh in range(N_HEADS)]
    decay_read = [jnp.exp(lg[hh] * (idx + 1.0)) for hh in range(N_HEADS)]
    decay_write = [jnp.exp(lg[hh] * (n - 1.0 - idx)) for hh in range(N_HEADS)]
    decay_all = [jnp.exp(jnp.full((1, HEAD_DIM), lg[hh] * n, F32)) for hh in range(N_HEADS)]
    pairs = [(st, hh) for st in range(q_ref.shape[0]) for hh in range(N_HEADS)]
    qs = [q_ref[st, :, cols[hh]] for st, hh in pairs]
    ks = [k_ref[st, :, cols[hh]] for st, hh in pairs]
    vs = [v_ref[st, :, cols[hh]] for st, hh in pairs]
    scores = [lax.dot_general(q, k, _NT, preferred_element_type=F32) for q, k in zip(qs, ks)]
    reads = [jnp.dot(q, s_ref[st, hh].astype(BF16), preferred_element_type=F32)
             for q, (st, hh) in zip(qs, pairs)]
    kws = [(k.astype(F32) * decay_write[hh]).astype(BF16) for k, (_, hh) in zip(ks, pairs)]
    writes = [lax.dot_general(kw, v, _TN, preferred_element_type=F32) for kw, v in zip(kws, vs)]
    weighted = [(sc * intra[hh]).astype(BF16) for sc, (_, hh) in zip(scores, pairs)]
    outs = [jnp.dot(w, v, preferred_element_type=F32) + rd * decay_read[hh]
            for w, v, rd, (_, hh) in zip(weighted, vs, reads, pairs)]
    for (st, hh), out, write in zip(pairs, outs, writes):
        so_ref[st, hh] = s_ref[st, hh] * decay_all[hh] + write
        o_ref[st, :, cols[hh]] = _head_norm_gate(
            out, g_ref[:, cols[hh]], gate_ref[st, :, cols[hh]]).astype(BF16)


def _ret_sample(log_gamma, p3, g_r, state):
    b, n, _ = p3.shape
    rs = RET_SAMPLE_STREAMS
    seg = lambda sidx: pl.BlockSpec((rs, n, GROUP_WIDTH), lambda b_, lg: (b_, 0, sidx))
    state_spec = pl.BlockSpec((rs, N_HEADS, HEAD_DIM, HEAD_DIM), lambda b_, lg: (b_, 0, 0, 0))
    return pl.pallas_call(
        _ret_sample_kernel,
        grid_spec=pltpu.PrefetchScalarGridSpec(
            num_scalar_prefetch=1,
            grid=(b // rs,),
            in_specs=[seg(4), seg(5), seg(6), seg(7),
                      pl.BlockSpec((1, GROUP_WIDTH), lambda b_, lg: (0, 0)),
                      state_spec],
            out_specs=[pl.BlockSpec((rs, n, GROUP_WIDTH), lambda b_, lg: (b_, 0, 0)), state_spec],
        ),
        out_shape=[
            jax.ShapeDtypeStruct((b, n, GROUP_WIDTH), BF16),
            jax.ShapeDtypeStruct(state.shape, F32),
        ],
        compiler_params=_params(1),
        name="ret_sample",
    )(log_gamma, p3, p3, p3, p3, g_r, state)


OUT_TM = 512


def _out_proj_kernel(ysb_ref, yr_ref, w_ref, g_ref, x_ref, o_ref):
    out = jnp.dot(ysb_ref[...], w_ref[0:GROUP_WIDTH, :], preferred_element_type=F32)
    out = out + jnp.dot(yr_ref[...], w_ref[GROUP_WIDTH:2 * GROUP_WIDTH, :], preferred_element_type=F32)
    r = lax.rsqrt(jnp.mean(out * out, axis=-1, keepdims=True) + EPS)
    o_ref[...] = x_ref[...] + out * r * g_ref[...]


def _out_proj(y_sb, y_r, w_bf16, g_post, x2d):
    m, d = x2d.shape
    tm = min(OUT_TM, m)
    row = lambda width: pl.BlockSpec((tm, width), lambda i: (i, 0))
    return pl.pallas_call(
        _out_proj_kernel,
        grid=(m // tm,),
        in_specs=[row(GROUP_WIDTH), row(GROUP_WIDTH),
                  pl.BlockSpec(w_bf16.shape, lambda i: (0, 0)),
                  pl.BlockSpec((1, d), lambda i: (0, 0)),
                  row(d)],
        out_specs=row(d),
        out_shape=jax.ShapeDtypeStruct((m, d), F32),
        compiler_params=_params(1),
        name="out_proj",
    )(y_sb, y_r, w_bf16, g_post, x2d)


def _epilogue_tables(positions):
    half = HEAD_DIM // 2
    inv = ROPE_BASE ** (-np.arange(half, dtype=np.float64) / half)
    ang = positions.astype(np.float64)[:, None] * inv[None, :]
    cos = np.concatenate([np.cos(ang), np.cos(ang)], axis=-1)
    sin = np.concatenate([-np.sin(ang), np.sin(ang)], axis=-1)
    one, zero = np.ones_like(cos), np.zeros_like(sin)
    cos4 = np.stack([one * Q_SCALE, one, cos, cos * Q_SCALE])
    sin4 = np.stack([zero, zero, sin, sin * Q_SCALE])
    return jnp.asarray(cos4, F32), jnp.asarray(sin4, F32)


def kernel(x_prompt, x_sample, cache_sb_k, cache_sb_v, state_ret, norm_pre, w_in, sb_head_norm,
           ret_head_norm, w_out, norm_post):
    depth = norm_pre.shape[0]
    assert depth == 1
    b_p, t_p, d = x_prompt.shape
    b_s, t_s, _ = x_sample.shape
    past = cache_sb_k.shape[2]
    log_gamma = jnp.asarray(np.log(1.0 - 2.0 ** (-5.0 - np.arange(N_HEADS))), F32)

    l = 0
    w_in_b = w_in[l].astype(BF16)
    w_out_b = w_out[l].astype(BF16)
    g_pre = norm_pre[l][None, :]
    g_post = norm_post[l][None, :]
    g_sb = sb_head_norm[l][None, :]
    g_r = ret_head_norm[l][None, :]

    cos_p, sin_p = _epilogue_tables(np.arange(t_p))
    xp2 = x_prompt.reshape(b_p * t_p, d)
    p_p, k_p, v_p = _in_proj(xp2, g_pre, w_in_b, cos_p, sin_p)
    p3 = p_p.reshape(b_p, t_p, N_SEGMENTS * GROUP_WIDTH)
    y_sb = _sb_prompt(p3, g_sb)
    y_r, s_p = _ret_prompt(log_gamma, p3, g_r)
    y_p = _out_proj(y_sb.reshape(b_p * t_p, GROUP_WIDTH), y_r.reshape(b_p * t_p, GROUP_WIDTH),
                    w_out_b, g_post, xp2).reshape(b_p, t_p, d)

    cos_s, sin_s = _epilogue_tables(np.tile(past + np.arange(t_s), b_s))
    xs2 = x_sample.reshape(b_s * t_s, d)
    p_s, k_s, v_s = _in_proj(xs2, g_pre, w_in_b, cos_s, sin_s)
    ps3 = p_s.reshape(b_s, t_s, N_SEGMENTS * GROUP_WIDTH)
    ys_sb = _sb_sample(ps3, g_sb, cache_sb_k[l].reshape(b_s, past * N_HEADS, HEAD_DIM),
                       cache_sb_v[l].reshape(b_s, past * N_HEADS, HEAD_DIM))
    ys_r, s_s = _ret_sample(log_gamma, ps3, g_r, state_ret[l])
    y_s = _out_proj(ys_sb.reshape(b_s * t_s, GROUP_WIDTH), ys_r.reshape(b_s * t_s, GROUP_WIDTH),
                    w_out_b, g_post, xs2).reshape(b_s, t_s, d)

    heads = lambda a, bb, tt: a.reshape(1, bb, tt, N_HEADS, HEAD_DIM)
    return (y_p, y_s, heads(k_p, b_p, t_p), heads(v_p, b_p, t_p), s_p[None],
            heads(k_s, b_s, t_s), heads(v_s, b_s, t_s), s_s[None])
```
